```python
import jax, jax.numpy as jnp
from jax import lax
import numpy as np


D_MODEL = 4096
BATCH = 4
SEQ = 2048
DEPTH = 4
DEC_BATCH = 128
DEC_SEQ = 4
PAST_LEN = 16384
PAGE_SIZE = 128

D_RNN = D_MODEL
N_RNN_BLOCKS = 16
RNN_BLOCK = D_RNN // N_RNN_BLOCKS
CONV_W = 4
LRU_C = 8.0
D_CHUNK = D_MODEL // 2
CHUNK = 128
N_SGU_GROUPS = 8
SGU_GROUP = D_CHUNK // N_SGU_GROUPS
D_FF = 3 * D_MODEL
N_EXPERTS = 8
TOP_K = 2
D_FF_EXPERT = 3 * D_MODEL // 2
EPS = 1e-6
N_DENSE = (DEPTH + 1) // 2
N_MOE = DEPTH // 2
D_IN = D_RNN + 2 * D_CHUNK + 2 * D_MODEL

kernel_name = "hawk_gmlp_parallel_moe_step"


def rmsnorm(x, g):
    xf = x.astype(jnp.float32)
    y = xf * lax.rsqrt(jnp.mean(xf * xf, axis=-1, keepdims=True) + EPS)
    return (y * g.astype(jnp.float32)).astype(x.dtype)


def layernorm(x, g, b):
    xf = x.astype(jnp.float32)
    mu = jnp.mean(xf, axis=-1, keepdims=True)
    var = jnp.mean(jnp.square(xf - mu), axis=-1, keepdims=True)
    y = (xf - mu) * lax.rsqrt(var + EPS)
    return (y * g.astype(jnp.float32) + b.astype(jnp.float32)).astype(x.dtype)


def causal_conv(x, buf, w, b):
    T = x.shape[1]
    xp = jnp.concatenate([buf.astype(x.dtype), x], axis=1)
    out = b
    for k in range(CONV_W):
        out = out + xp[:, k:k + T] * w[k]
    new_buf = xp[:, xp.shape[1] - (CONV_W - 1):]
    return out.astype(x.dtype), new_buf


def block_diag(x, w, b):
    B, T, _ = x.shape
    xb = x.reshape(B, T, N_RNN_BLOCKS, RNN_BLOCK)
    y = jnp.einsum('btnr,nrs->btns', xb, w) + b
    return y.reshape(B, T, D_RNN)


def rglru(x, h0, pos0, wa, ba, wx, bx, lam):
    T = x.shape[1]
    xf = x.astype(jnp.float32)
    r = jax.nn.sigmoid(block_diag(xf, wa, ba))
    i = jax.nn.sigmoid(block_diag(xf, wx, bx))
    log_a = LRU_C * r * jax.nn.log_sigmoid(lam.astype(jnp.float32))
    a = jnp.exp(log_a)
    mult = jnp.sqrt(-jnp.expm1(2.0 * log_a))
    pos = pos0 + jnp.arange(T)
    mult = jnp.where((pos == 0)[None, :, None], 1.0, mult)
    bx_in = mult * (i * xf)

    def step(h, ab):
        a_t, b_t = ab
        h = a_t * h + b_t
        return h, h

    h_last, ys = lax.scan(step, h0.astype(jnp.float32),
                          (jnp.swapaxes(a, 0, 1), jnp.swapaxes(bx_in, 0, 1)))
    return jnp.swapaxes(ys, 0, 1).astype(x.dtype), h_last.astype(h0.dtype)


def sgu(u, v, ln_g, ln_b, ws, bs, sgu_len):
    B, T, _ = u.shape
    vn = layernorm(v, ln_g, ln_b)
    vc = vn.reshape(B, T // sgu_len, sgu_len, N_SGU_GROUPS, SGU_GROUP)
    mask = jnp.tril(jnp.ones((sgu_len, sgu_len), dtype=bool))
    w = jnp.where(mask[None], ws[:, :sgu_len, :sgu_len], 0.0).astype(vc.dtype)
    s = jnp.einsum('gts,bnsgc->bntgc', w, vc)
    s = s + jnp.transpose(bs[:, :sgu_len])[None, None, :, :, None]
    return u * s.reshape(B, T, D_CHUNK), vn


def mixer(x, h0, buf, pos0, sgu_len, g, w_in, conv_w, conv_b, wa, ba, wx, bx, lam,
          ln_g, ln_b, ws, bs, w_br_a, w_br_b, w_out):
    xn = rmsnorm(x, g)
    z = jnp.einsum('btd,de->bte', xn, w_in)
    o1 = D_RNN
    o2 = o1 + D_CHUNK
    o3 = o2 + D_CHUNK
    o4 = o3 + D_MODEL
    xr = z[..., :o1]
    u = jax.nn.gelu(z[..., o1:o2])
    v = jax.nn.gelu(z[..., o2:o3])
    g_a = z[..., o3:o4]
    g_b = z[..., o4:]
    xc, new_buf = causal_conv(xr, buf, conv_w, conv_b)
    y_a, h_last = rglru(xc, h0, pos0, wa, ba, wx, bx, lam)
    y_b, vn = sgu(u, v, ln_g, ln_b, ws, bs, sgu_len)
    m = (jax.nn.sigmoid(g_a) * jnp.einsum('btc,cd->btd', y_a, w_br_a)
         + jax.nn.sigmoid(g_b) * jnp.einsum('btc,cd->btd', y_b, w_br_b))
    return jnp.einsum('btd,de->bte', m, w_out), h_last, new_buf, vn


def swiglu(x, wg, wu, wd):
    h = jax.nn.silu(jnp.einsum('btd,df->btf', x, wg)) * jnp.einsum('btd,df->btf', x, wu)
    return jnp.einsum('btf,fd->btd', h, wd)


def moe(x, w_router, wg, wu, wd):
    logits = jnp.einsum('btd,de->bte', x, w_router).astype(jnp.float32)
    vals, idx = lax.top_k(logits, TOP_K)
    wts = jax.nn.softmax(vals, axis=-1)
    gate = jnp.sum(jax.nn.one_hot(idx, N_EXPERTS, dtype=jnp.float32) * wts[..., None], axis=-2)
    gate = gate.astype(x.dtype)
    out = jnp.zeros_like(x)
    for e in range(N_EXPERTS):
        out = out + gate[..., e:e + 1] * swiglu(x, wg[e], wu[e], wd[e])
    return out


def setup_inputs(seed: int = 0) -> dict:
    key = jax.random.key(seed)
    ks = jax.random.split(key, 32)
    f32 = jnp.float32

    def nrm(k, shape, scale):
        return jax.random.normal(k, shape, f32) * scale

    a0 = jax.random.uniform(ks[9], (DEPTH, D_RNN), f32, 0.9, 0.999)
    s0 = a0 ** (1.0 / LRU_C)
    lam = jnp.log(s0) - jnp.log1p(-s0)
    return {
        "x_prompt": nrm(ks[0], (BATCH, SEQ, D_MODEL), 1.0),
        "x_sample": nrm(ks[1], (DEC_BATCH, DEC_SEQ, D_MODEL), 1.0),
        "state_rglru_h": nrm(ks[2], (DEPTH, DEC_BATCH, D_RNN), 0.5),
        "state_conv": nrm(ks[3], (DEPTH, DEC_BATCH, CONV_W - 1, D_RNN), 1.0),
        "norm_mix_g": 1.0 + nrm(ks[4], (DEPTH, D_MODEL), 0.02),
        "w_in": nrm(ks[5], (DEPTH, D_MODEL, D_IN), D_MODEL ** -0.5),
        "conv_w": nrm(ks[6], (DEPTH, CONV_W, D_RNN), CONV_W ** -0.5),
        "conv_b": nrm(ks[7], (DEPTH, D_RNN), 0.02),
        "lru_wa": nrm(ks[8], (DEPTH, N_RNN_BLOCKS, RNN_BLOCK, RNN_BLOCK), RNN_BLOCK ** -0.5),
        "lru_ba": nrm(ks[10], (DEPTH, N_RNN_BLOCKS, RNN_BLOCK), 0.02),
        "lru_wx": nrm(ks[11], (DEPTH, N_RNN_BLOCKS, RNN_BLOCK, RNN_BLOCK), RNN_BLOCK ** -0.5),
        "lru_bx": nrm(ks[12], (DEPTH, N_RNN_BLOCKS, RNN_BLOCK), 0.02),
        "lru_lambda": lam,
        "sgu_ln_g": 1.0 + nrm(ks[13], (DEPTH, D_CHUNK), 0.02),
        "sgu_ln_b": nrm(ks[14], (DEPTH, D_CHUNK), 0.02),
        "sgu_ws": nrm(ks[15], (DEPTH, N_SGU_GROUPS, CHUNK, CHUNK), 0.5 * CHUNK ** -0.5),
        "sgu_bs": 1.0 + nrm(ks[16], (DEPTH, N_SGU_GROUPS, CHUNK), 0.02),
        "w_br_a": nrm(ks[17], (DEPTH, D_RNN, D_MODEL), D_RNN ** -0.5),
        "w_br_b": nrm(ks[18], (DEPTH, D_CHUNK, D_MODEL), D_CHUNK ** -0.5),
        "w_out": nrm(ks[19], (DEPTH, D_MODEL, D_MODEL), D_MODEL ** -0.5),
        "norm_ffn_g": 1.0 + nrm(ks[20], (DEPTH, D_MODEL), 0.02),
        "ffn_wg": nrm(ks[21], (N_DENSE, D_MODEL, D_FF), D_MODEL ** -0.5),
        "ffn_wu": nrm(ks[22], (N_DENSE, D_MODEL, D_FF), D_MODEL ** -0.5),
        "ffn_wd": nrm(ks[23], (N_DENSE, D_FF, D_MODEL), D_FF ** -0.5),
        "router_w": nrm(ks[24], (N_MOE, D_MODEL, N_EXPERTS), D_MODEL ** -0.5),
        "moe_wg": nrm(ks[25], (N_MOE, N_EXPERTS, D_MODEL, D_FF_EXPERT), D_MODEL ** -0.5),
        "moe_wu": nrm(ks[26], (N_MOE, N_EXPERTS, D_MODEL, D_FF_EXPERT), D_MODEL ** -0.5),
        "moe_wd": nrm(ks[27], (N_MOE, N_EXPERTS, D_FF_EXPERT, D_MODEL), D_FF_EXPERT ** -0.5),
        "norm_final_g": 1.0 + nrm(ks[28], (D_MODEL,), 0.02),
    }


def reference(x_prompt, x_sample, state_rglru_h, state_conv, norm_mix_g, w_in, conv_w, conv_b,
              lru_wa, lru_ba, lru_wx, lru_bx, lru_lambda, sgu_ln_g, sgu_ln_b, sgu_ws, sgu_bs,
              w_br_a, w_br_b, w_out, norm_ffn_g, ffn_wg, ffn_wu, ffn_wd, router_w,
              moe_wg, moe_wu, moe_wd, norm_final_g):
    xp = x_prompt
    xs = x_sample
    h_p_list, c_p_list, h_s_list, c_s_list, v_s_list = [], [], [], [], []
    h0_p = jnp.zeros((BATCH, D_RNN), x_prompt.dtype)
    buf0_p = jnp.zeros((BATCH, CONV_W - 1, D_RNN), x_prompt.dtype)
    for l in range(DEPTH):
        p = (norm_mix_g[l], w_in[l], conv_w[l], conv_b[l], lru_wa[l], lru_ba[l], lru_wx[l],
             lru_bx[l], lru_lambda[l], sgu_ln_g[l], sgu_ln_b[l], sgu_ws[l], sgu_bs[l],
             w_br_a[l], w_br_b[l], w_out[l])
        o_p, h_p, c_p, _ = mixer(xp, h0_p, buf0_p, 0, CHUNK, *p)
        o_s, h_s, c_s, v_s = mixer(xs, state_rglru_h[l], state_conv[l], PAST_LEN, DEC_SEQ, *p)
        xp = xp + o_p
        xs = xs + o_s
        h_p_list.append(h_p)
        c_p_list.append(c_p)
        h_s_list.append(h_s)
        c_s_list.append(c_s)
        v_s_list.append(v_s)
        xnp = rmsnorm(xp, norm_ffn_g[l])
        xns = rmsnorm(xs, norm_ffn_g[l])
        j = l // 2
        if l % 2 == 0:
            xp = xp + swiglu(xnp, ffn_wg[j], ffn_wu[j], ffn_wd[j])
            xs = xs + swiglu(xns, ffn_wg[j], ffn_wu[j], ffn_wd[j])
        else:
            xp = xp + moe(xnp, router_w[j], moe_wg[j], moe_wu[j], moe_wd[j])
            xs = xs + moe(xns, router_w[j], moe_wg[j], moe_wu[j], moe_wd[j])
    y_prompt = rmsnorm(xp, norm_final_g)
    y_sample = rmsnorm(xs, norm_final_g)
    new_h_prompt = jnp.stack(h_p_list, axis=0)
    new_conv_prompt = jnp.stack(c_p_list, axis=0)
    new_h_sample = jnp.stack(h_s_list, axis=0)
    new_conv_sample = jnp.stack(c_s_list, axis=0)
    new_v_sample = jnp.stack(v_s_list, axis=0)
    return (y_prompt, y_sample, new_h_prompt, new_conv_prompt, new_h_sample, new_conv_sample, new_v_sample)
```

```python
import functools

import jax
import jax.numpy as jnp
from jax import lax
from jax.experimental import pallas as pl
from jax.experimental.pallas import tpu as pltpu

F32 = jnp.float32
BF16 = jnp.bfloat16

LRU_C = 8.0
EPS = 1e-6
PAST_LEN = 16384
TOP_K = 2

V7X_VMEM_BYTES = 64 * 1024 * 1024
VMEM_LIMIT_BYTES = V7X_VMEM_BYTES - 6 * 1024 * 1024
BF16_ROWS = 16
LANES = 128

ROW_TILE = 1088
ROW_TILE_HALF = 544
COL_TILE = 512
K_CHUNK = 4096
GATHER_TILE = 272
SCAN_CHUNK = 256
POS_TILE = 512


def _params(*sem):
    return pltpu.CompilerParams(dimension_semantics=sem, vmem_limit_bytes=VMEM_LIMIT_BYTES)


def _tile(n, target, mult):
    for t in range(min(n, target), 0, -1):
        if n % t == 0 and t % mult == 0:
            return t
    raise ValueError(f"no tile for {n} (target {target}, multiple {mult})")


def _rms(x, g):
    return x * lax.rsqrt(jnp.mean(x * x, axis=-1, keepdims=True) + EPS) * g


def _norm_kernel(x_ref, g_ref, o_ref):
    o_ref[...] = _rms(x_ref[...], g_ref[...]).astype(o_ref.dtype)


def _rmsnorm(x, g3, layer, tm, out_dtype, row_block0=0, n_rows=None):
    m, d = x.shape
    n_rows = m if n_rows is None else n_rows
    return pl.pallas_call(
        _norm_kernel,
        grid=(n_rows // tm,),
        in_specs=[pl.BlockSpec((tm, d), lambda i: (i + row_block0, 0)),
                  pl.BlockSpec((None, 1, d), lambda i: (layer, 0, 0))],
        out_specs=pl.BlockSpec((tm, d), lambda i: (i, 0)),
        out_shape=jax.ShapeDtypeStruct((n_rows, d), out_dtype),
        compiler_params=_params("parallel"),
        name="rmsnorm",
    )(x, g3)


def _inproj_kernel(n_lin, n_gelu, x_ref, w_ref, o_ref, wbf_ref):
    j = pl.program_id(0)

    @pl.when(pl.program_id(1) == 0)
    def _():
        wbf_ref[...] = w_ref[...].astype(BF16)

    acc = jnp.dot(x_ref[...], wbf_ref[...], preferred_element_type=F32)

    @pl.when(j < n_lin)
    def _():
        o_ref[...] = acc.astype(o_ref.dtype)

    @pl.when((j >= n_lin) & (j < n_lin + n_gelu))
    def _():
        o_ref[...] = jax.nn.gelu(acc, approximate=True).astype(o_ref.dtype)

    @pl.when(j >= n_lin + n_gelu)
    def _():
        o_ref[...] = jax.nn.sigmoid(acc).astype(o_ref.dtype)


def _inproj(xn, w_in, layer, d_rnn, d_chunk):
    m, d = xn.shape
    n = w_in.shape[-1]
    tm = _tile(m, ROW_TILE, BF16_ROWS)
    tn = _tile(d_chunk, COL_TILE, LANES)
    kern = functools.partial(_inproj_kernel, d_rnn // tn, 2 * d_chunk // tn)
    return pl.pallas_call(
        kern,
        grid=(n // tn, m // tm),
        in_specs=[pl.BlockSpec((tm, d), lambda j, i: (i, 0)),
                  pl.BlockSpec((None, d, tn), lambda j, i: (layer, 0, j))],
        out_specs=pl.BlockSpec((tm, tn), lambda j, i: (i, j)),
        out_shape=jax.ShapeDtypeStruct((m, n), F32),
        scratch_shapes=[pltpu.VMEM((d, tn), BF16)],
        compiler_params=_params("arbitrary", "arbitrary"),
        name="in_proj",
    )(xn, w_in)


def _merge_kernel(n_pt, yap_ref, ybp_ref, yas_ref, ybs_ref, wa_ref, wb_ref, sa_ref, sb_ref, o_ref,
                  wabf_ref, wbbf_ref):
    i = pl.program_id(1)

    @pl.when(i == 0)
    def _():
        wabf_ref[...] = wa_ref[...].astype(BF16)
        wbbf_ref[...] = wb_ref[...].astype(BF16)

    def tile(ya_ref, yb_ref):
        pa = jnp.dot(ya_ref[...], wabf_ref[...], preferred_element_type=F32)
        pb = jnp.dot(yb_ref[...], wbbf_ref[...], preferred_element_type=F32)
        o_ref[...] = (sa_ref[...].astype(F32) * pa + sb_ref[...].astype(F32) * pb).astype(o_ref.dtype)

    @pl.when(i < n_pt)
    def _():
        tile(yap_ref, ybp_ref)

    @pl.when(i >= n_pt)
    def _():
        tile(yas_ref, ybs_ref)


def _merge(ya_p, yb_p, ya_s, yb_s, z, w_br_a, w_br_b, layer, gate_col0):
    n_prompt, da = ya_p.shape
    tm = ya_s.shape[0]
    db = yb_p.shape[1]
    d = w_br_a.shape[-1]
    n_pt = n_prompt // tm
    tn = _tile(d, COL_TILE, LANES)
    ga0 = gate_col0 // tn
    gb0 = (gate_col0 + d) // tn
    prompt_rows = lambda j, i: (jnp.minimum(i, n_pt - 1), 0)
    once = pl.Buffered(1)
    return pl.pallas_call(
        functools.partial(_merge_kernel, n_pt),
        grid=(d // tn, n_pt + 1),
        in_specs=[pl.BlockSpec((tm, da), prompt_rows),
                  pl.BlockSpec((tm, db), prompt_rows),
                  pl.BlockSpec((tm, da), lambda j, i: (0, 0), pipeline_mode=once),
                  pl.BlockSpec((tm, db), lambda j, i: (0, 0), pipeline_mode=once),
                  pl.BlockSpec((None, da, tn), lambda j, i: (layer, 0, j)),
                  pl.BlockSpec((None, db, tn), lambda j, i: (layer, 0, j)),
                  pl.BlockSpec((tm, tn), lambda j, i: (i, ga0 + j)),
                  pl.BlockSpec((tm, tn), lambda j, i: (i, gb0 + j))],
        out_specs=pl.BlockSpec((tm, tn), lambda j, i: (i, j)),
        out_shape=jax.ShapeDtypeStruct((n_prompt + tm, d), BF16),
        scratch_shapes=[pltpu.VMEM((da, tn), BF16), pltpu.VMEM((db, tn), BF16)],
        compiler_params=_params("arbitrary", "arbitrary"),
        name="branch_merge",
    )(ya_p, yb_p, ya_s, yb_s, w_br_a, w_br_b, z, z)


def _matmul_acc_kernel(x_ref, w_ref, acc_ref, o_ref, wbf_ref):
    @pl.when(pl.program_id(1) == 0)
    def _():
        wbf_ref[...] = w_ref[...].astype(BF16)

    o_ref[...] = acc_ref[...] + jnp.dot(x_ref[...], wbf_ref[...], preferred_element_type=F32)


def _matmul_acc(x, w, layer, acc, k_block=0):
    m = x.shape[0]
    n = w.shape[-1]
    tk = min(K_CHUNK, w.shape[-2])
    tm = _tile(m, ROW_TILE, BF16_ROWS)
    tn = _tile(n, COL_TILE, LANES)
    return pl.pallas_call(
        _matmul_acc_kernel,
        grid=(n // tn, m // tm),
        in_specs=[pl.BlockSpec((tm, tk), lambda j, i: (i, k_block)),
                  pl.BlockSpec((None, tk, tn), lambda j, i: (layer, k_block, j)),
                  pl.BlockSpec((tm, tn), lambda j, i: (i, j))],
        out_specs=pl.BlockSpec((tm, tn), lambda j, i: (i, j)),
        out_shape=jax.ShapeDtypeStruct((m, n), F32),
        scratch_shapes=[pltpu.VMEM((tk, tn), BF16)],
        input_output_aliases={2: 0},
        compiler_params=_params("arbitrary", "arbitrary"),
        name="matmul_acc",
    )(x, w, acc)


def _gateup_kernel(x_ref, wg_ref, wu_ref, o_ref, wgbf_ref, wubf_ref):
    @pl.when(pl.program_id(1) == 0)
    def _():
        wgbf_ref[...] = wg_ref[...].astype(BF16)
        wubf_ref[...] = wu_ref[...].astype(BF16)

    x = x_ref[...]
    g = jnp.dot(x, wgbf_ref[...], preferred_element_type=F32)
    u = jnp.dot(x, wubf_ref[...], preferred_element_type=F32)
    o_ref[...] = (jax.nn.silu(g) * u).astype(o_ref.dtype)


def _gateup(xn, wg, wu, layer):
    m, d = xn.shape
    f = wg.shape[-1]
    tm = _tile(m, ROW_TILE_HALF, BF16_ROWS)
    tf = _tile(f, COL_TILE, LANES)
    return pl.pallas_call(
        _gateup_kernel,
        grid=(f // tf, m // tm),
        in_specs=[pl.BlockSpec((tm, d), lambda j, i: (i, 0)),
                  pl.BlockSpec((None, d, tf), lambda j, i: (layer, 0, j)),
                  pl.BlockSpec((None, d, tf), lambda j, i: (layer, 0, j))],
        out_specs=pl.BlockSpec((tm, tf), lambda j, i: (i, j)),
        out_shape=jax.ShapeDtypeStruct((m, f), BF16),
        scratch_shapes=[pltpu.VMEM((d, tf), BF16), pltpu.VMEM((d, tf), BF16)],
        compiler_params=_params("arbitrary", "arbitrary"),
        name="ffn_gate_up",
    )(xn, wg, wu)


def _lru_gates(xc, wa_bf, wx_bf, ba, bx, logsig):
    xb = xc.astype(BF16)
    r = jax.nn.sigmoid(jnp.dot(xb, wa_bf, preferred_element_type=F32) + ba)
    i = jax.nn.sigmoid(jnp.dot(xb, wx_bf, preferred_element_type=F32) + bx)
    log_a = LRU_C * r * logsig
    a = jnp.exp(log_a)
    th = jnp.tanh(log_a)
    mult = jnp.sqrt(-2.0 * th / (1.0 - th))
    return a, mult, i * xc


def _scan8(a, b, h_prev, row):
    for d in (1, 2, 4):
        keep = row >= d
        a_s = jnp.where(keep, pltpu.roll(a, d, 0), 1.0)
        b_s = jnp.where(keep, pltpu.roll(b, d, 0), 0.0)
        b = b + a * b_s
        a = a * a_s
    y = b + a * h_prev
    return y, y[7:8, :]


def _rglru_prompt_kernel(conv_w, z_ref, cw_ref, cb_ref, wa_ref, wx_ref, ba_ref, bx_ref, lam_ref,
                         y_ref, h_ref, nb_ref, xp_ref, a_ref, b_ref):
    t_len, c = z_ref.shape
    pad = 8
    xp_ref[0:pad, :] = jnp.zeros((pad, c), F32)
    xp_ref[pad:pad + t_len, :] = z_ref[...].astype(F32)
    nb_ref[...] = xp_ref[pad + t_len - (conv_w - 1):pad + t_len, :]

    wa_bf = wa_ref[...].astype(BF16)
    wx_bf = wx_ref[...].astype(BF16)
    logsig = jax.nn.log_sigmoid(lam_ref[...])
    ba = ba_ref[...]
    bx = bx_ref[...]
    cb = cb_ref[...]
    row8 = lax.broadcasted_iota(jnp.int32, (8, c), 0)
    ch = min(SCAN_CHUNK, t_len)

    h = jnp.zeros((1, c), F32)
    for ci in range(t_len // ch):
        base = pad + ci * ch
        xc = cb
        for k in range(conv_w):
            off = base - (conv_w - 1) + k
            xc = xc + cw_ref[k:k + 1, :] * xp_ref[off:off + ch, :]
        a, mult, ix = _lru_gates(xc, wa_bf, wx_bf, ba, bx, logsig)
        if ci == 0:
            rows = lax.broadcasted_iota(jnp.int32, (ch, c), 0)
            mult = jnp.where(rows == 0, 1.0, mult)
        a_ref[...] = a
        b_ref[...] = mult * ix

        def group(g, h, ci=ci):
            r0 = pl.multiple_of(g * BF16_ROWS, BF16_ROWS)
            a16 = a_ref[pl.ds(r0, BF16_ROWS), :]
            b16 = b_ref[pl.ds(r0, BF16_ROWS), :]
            y0, h = _scan8(a16[0:8], b16[0:8], h, row8)
            y1, h = _scan8(a16[8:16], b16[8:16], h, row8)
            y_ref[pl.ds(ci * ch + r0, BF16_ROWS), :] = jnp.concatenate([y0, y1], 0).astype(y_ref.dtype)
            return h

        h = lax.fori_loop(0, ch // BF16_ROWS, group, h)
    h_ref[...] = h


def _rglru_prompt(z, p, layer, batch, seq, rb):
    d_rnn = p["conv_b"].shape[-1]
    conv_w = p["conv_w"].shape[1]
    nb = d_rnn // rb
    vec = lambda: pl.BlockSpec((None, 1, rb), lambda b, n: (layer, 0, n))
    blk = lambda: pl.BlockSpec((None, None, rb, rb), lambda b, n: (layer, n, 0, 0))
    ch = min(SCAN_CHUNK, seq)
    return pl.pallas_call(
        functools.partial(_rglru_prompt_kernel, conv_w),
        grid=(batch, nb),
        in_specs=[pl.BlockSpec((seq, rb), lambda b, n: (b, n)),
                  pl.BlockSpec((None, conv_w, rb), lambda b, n: (layer, 0, n)),
                  vec(), blk(), blk(), vec(), vec(), vec()],
        out_specs=[pl.BlockSpec((seq, rb), lambda b, n: (b, n)),
                   pl.BlockSpec((None, 1, rb), lambda b, n: (b, 0, n)),
                   pl.BlockSpec((None, conv_w - 1, rb), lambda b, n: (b, 0, n))],
        out_shape=[jax.ShapeDtypeStruct((batch * seq, d_rnn), BF16),
                   jax.ShapeDtypeStruct((batch, 1, d_rnn), F32),
                   jax.ShapeDtypeStruct((batch, conv_w - 1, d_rnn), F32)],
        scratch_shapes=[pltpu.VMEM((seq + 8, rb), F32),
                        pltpu.VMEM((ch, rb), F32), pltpu.VMEM((ch, rb), F32)],
        compiler_params=_params("parallel", "parallel"),
        name="rglru_prompt",
    )(z, p["conv_w"], p["conv_b"], p["lru_wa"], p["lru_wx"], p["lru_ba"], p["lru_bx"], p["lru_lambda"])


def _rglru_sample_kernel(conv_w, n_t, z_ref, buf_ref, h0_ref, cw_ref, cb_ref, wa_ref, wx_ref,
                         ba_ref, bx_ref, lam_ref, y_ref, h_ref, nb_ref):
    rows, c = z_ref.shape
    bsz = rows // n_t
    xr = z_ref[...].astype(F32)
    slabs = [buf_ref[k] for k in range(conv_w - 1)] + [xr[t * bsz:(t + 1) * bsz] for t in range(n_t)]
    cb = cb_ref[...]
    xcs = []
    for t in range(n_t):
        xc = cb
        for k in range(conv_w):
            xc = xc + cw_ref[k:k + 1, :] * slabs[t + k]
        xcs.append(xc)
    xc = jnp.concatenate(xcs, 0)
    a, mult, ix = _lru_gates(xc, wa_ref[...].astype(BF16), wx_ref[...].astype(BF16),
                             ba_ref[...], bx_ref[...], jax.nn.log_sigmoid(lam_ref[...]))
    b = mult * ix
    h = h0_ref[...]
    ys = []
    for t in range(n_t):
        h = a[t * bsz:(t + 1) * bsz] * h + b[t * bsz:(t + 1) * bsz]
        ys.append(h)
    y_ref[...] = jnp.concatenate(ys, 0).astype(y_ref.dtype)
    h_ref[...] = h
    for k in range(conv_w - 1):
        nb_ref[k] = slabs[n_t + k]


def _rglru_sample(z, buf_t, h0, p, layer, n_t, bsz, rb, row_block):
    d_rnn = p["conv_b"].shape[-1]
    conv_w = p["conv_w"].shape[1]
    assert PAST_LEN > 0 and n_t >= conv_w - 1
    rows = n_t * bsz
    vec = lambda: pl.BlockSpec((None, 1, rb), lambda n: (layer, 0, n))
    blk = lambda: pl.BlockSpec((None, None, rb, rb), lambda n: (layer, n, 0, 0))
    return pl.pallas_call(
        functools.partial(_rglru_sample_kernel, conv_w, n_t),
        grid=(d_rnn // rb,),
        in_specs=[pl.BlockSpec((rows, rb), lambda n: (row_block, n)),
                  pl.BlockSpec((None, conv_w - 1, bsz, rb), lambda n: (layer, 0, 0, n)),
                  pl.BlockSpec((None, bsz, rb), lambda n: (layer, 0, n)),
                  pl.BlockSpec((None, conv_w, rb), lambda n: (layer, 0, n)),
                  vec(), blk(), blk(), vec(), vec(), vec()],
        out_specs=[pl.BlockSpec((rows, rb), lambda n: (0, n)),
                   pl.BlockSpec((bsz, rb), lambda n: (0, n)),
                   pl.BlockSpec((conv_w - 1, bsz, rb), lambda n: (0, 0, n))],
        out_shape=[jax.ShapeDtypeStruct((rows, d_rnn), BF16),
                   jax.ShapeDtypeStruct((bsz, d_rnn), F32),
                   jax.ShapeDtypeStruct((conv_w - 1, bsz, d_rnn), F32)],
        compiler_params=_params("parallel"),
        name="rglru_sample",
    )(z, buf_t, h0, p["conv_w"], p["conv_b"], p["lru_wa"], p["lru_wx"], p["lru_ba"], p["lru_bx"],
      p["lru_lambda"])


def _layernorm(v, g, b):
    mu = jnp.mean(v, axis=-1, keepdims=True)
    var = jnp.mean(jnp.square(v - mu), axis=-1, keepdims=True)
    return (v - mu) * lax.rsqrt(var + EPS) * g + b


def _sgu_prompt_kernel(u_ref, v_ref, g_ref, b_ref, ws_ref, bst_ref, y_ref):
    rows = u_ref.shape[0]
    n_g, chunk, _ = ws_ref.shape
    sg = u_ref.shape[1] // n_g
    vn = _layernorm(v_ref[...].astype(F32), g_ref[...], b_ref[...]).astype(BF16)
    tri = (lax.broadcasted_iota(jnp.int32, (chunk, chunk), 0)
           >= lax.broadcasted_iota(jnp.int32, (chunk, chunk), 1))
    for g in range(n_g):
        w = jnp.where(tri, ws_ref[g], 0.0).astype(BF16)
        bias = bst_ref[:, g:g + 1]
        for ci in range(rows // chunk):
            rs = slice(ci * chunk, (ci + 1) * chunk)
            cs = slice(g * sg, (g + 1) * sg)
            s = jnp.dot(w, vn[rs, cs], preferred_element_type=F32) + bias
            y_ref[rs, cs] = (u_ref[rs, cs].astype(F32) * s).astype(y_ref.dtype)


def _sgu_prompt(z, p, layer, n_prompt, col_u, col_v):
    d_chunk = p["sgu_ln_g"].shape[-1]
    n_g, chunk = p["sgu_ws"].shape[1], p["sgu_ws"].shape[2]
    tm = _tile(n_prompt, POS_TILE, chunk)
    vec = lambda: pl.BlockSpec((None, 1, d_chunk), lambda i: (layer, 0, 0))
    return pl.pallas_call(
        _sgu_prompt_kernel,
        grid=(n_prompt // tm,),
        in_specs=[pl.BlockSpec((tm, d_chunk), lambda i: (i, col_u // d_chunk)),
                  pl.BlockSpec((tm, d_chunk), lambda i: (i, col_v // d_chunk)),
                  vec(), vec(),
                  pl.BlockSpec((None, n_g, chunk, chunk), lambda i: (layer, 0, 0, 0)),
                  pl.BlockSpec((None, chunk, n_g), lambda i: (layer, 0, 0))],
        out_specs=pl.BlockSpec((tm, d_chunk), lambda i: (i, 0)),
        out_shape=jax.ShapeDtypeStruct((n_prompt, d_chunk), BF16),
        compiler_params=_params("parallel"),
        name="sgu_prompt",
    )(z, z, p["sgu_ln_g"], p["sgu_ln_b"], p["sgu_ws"], p["sgu_bs_t"])


def _sgu_sample_kernel(n_t, u_ref, v_ref, g_ref, b_ref, coef_ref, bias_ref, y_ref, vn_ref):
    bsz = u_ref.shape[0] // n_t
    vn = _layernorm(v_ref[...].astype(F32), g_ref[...], b_ref[...])
    vn_ref[...] = vn
    vn = vn.astype(BF16).astype(F32)
    coef = coef_ref[...].astype(BF16).astype(F32)
    for t in range(n_t):
        s = bias_ref[t:t + 1, :]
        for k in range(t + 1):
            s = s + coef[t * n_t + k:t * n_t + k + 1, :] * vn[k * bsz:(k + 1) * bsz]
        rs = slice(t * bsz, (t + 1) * bsz)
        y_ref[rs, :] = (u_ref[rs, :].astype(F32) * s).astype(y_ref.dtype)


def _sgu_sample(z, p, layer, n_t, bsz, row_block, col_u, col_v):
    d_chunk = p["sgu_ln_g"].shape[-1]
    rows = n_t * bsz
    vec = lambda: pl.BlockSpec((None, 1, d_chunk), lambda i: (layer, 0, 0))
    return pl.pallas_call(
        functools.partial(_sgu_sample_kernel, n_t),
        grid=(1,),
        in_specs=[pl.BlockSpec((rows, d_chunk), lambda i: (row_block, col_u // d_chunk)),
                  pl.BlockSpec((rows, d_chunk), lambda i: (row_block, col_v // d_chunk)),
                  vec(), vec(),
                  pl.BlockSpec((None, n_t * n_t, d_chunk), lambda i: (layer, 0, 0)),
                  pl.BlockSpec((None, n_t, d_chunk), lambda i: (layer, 0, 0))],
        out_specs=[pl.BlockSpec((rows, d_chunk), lambda i: (0, 0)),
                   pl.BlockSpec((rows, d_chunk), lambda i: (0, 0))],
        out_shape=[jax.ShapeDtypeStruct((rows, d_chunk), BF16),
                   jax.ShapeDtypeStruct((rows, d_chunk), F32)],
        compiler_params=_params("arbitrary"),
        name="sgu_sample",
    )(z, z, p["sgu_ln_g"], p["sgu_ln_b"], p["sgu_coef"], p["sgu_bias"])


def _router_kernel(n_e, x_ref, g_ref, wr_ref, idx_ref, wt_ref):
    xn = _rms(x_ref[...], g_ref[...])
    logits = jnp.dot(xn.astype(BF16), wr_ref[...].astype(BF16), preferred_element_type=F32)
    lane = lax.broadcasted_iota(jnp.int32, logits.shape, 1)
    neg = jnp.float32(-jnp.inf)
    l1 = jnp.where(lane < n_e, logits, neg)
    m1 = jnp.max(l1, axis=-1, keepdims=True)
    i1 = jnp.min(jnp.where(l1 == m1, lane, LANES), axis=-1, keepdims=True)
    l2 = jnp.where(lane == i1, neg, l1)
    m2 = jnp.max(l2, axis=-1, keepdims=True)
    i2 = jnp.min(jnp.where(l2 == m2, lane, LANES), axis=-1, keepdims=True)
    e2 = jnp.exp(m2 - m1)
    w1 = 1.0 / (1.0 + e2)
    w2 = e2 / (1.0 + e2)
    idx_ref[...] = jnp.where(lane == 0, i1, jnp.where(lane == 1, i2, 0))
    wt_ref[...] = jnp.where(lane == 0, w1, jnp.where(lane == 1, w2, 0.0))


def _router(x, g3, layer, wr_pad, jm, n_e):
    m, d = x.shape
    tm = _tile(m, ROW_TILE_HALF, 8)
    return pl.pallas_call(
        functools.partial(_router_kernel, n_e),
        grid=(m // tm,),
        in_specs=[pl.BlockSpec((tm, d), lambda i: (i, 0)),
                  pl.BlockSpec((None, 1, d), lambda i: (layer, 0, 0)),
                  pl.BlockSpec((None, d, LANES), lambda i: (jm, 0, 0))],
        out_specs=[pl.BlockSpec((tm, LANES), lambda i: (i, 0)),
                   pl.BlockSpec((tm, LANES), lambda i: (i, 0))],
        out_shape=[jax.ShapeDtypeStruct((m, LANES), jnp.int32),
                   jax.ShapeDtypeStruct((m, LANES), F32)],
        compiler_params=_params("parallel"),
        name="moe_router",
    )(x, g3, wr_pad)


def _route_meta(idx2, n_e, te_rows, n_tiles):
    e = idx2.reshape(-1)
    n_pairs = e.shape[0]
    onehot = (e[:, None] == jnp.arange(n_e, dtype=jnp.int32)[None, :]).astype(jnp.int32)
    rank = jnp.sum((jnp.cumsum(onehot, axis=0) - onehot) * onehot, axis=1)
    count = jnp.sum(onehot, axis=0)
    tiles_e = (count + te_rows - 1) // te_rows
    tile_end = jnp.cumsum(tiles_e)
    tile_start = tile_end - tiles_e
    pos = tile_start[e] * te_rows + rank
    n_used = tile_end[-1:].astype(jnp.int32)
    token_of = jnp.zeros((n_tiles * te_rows,), jnp.int32).at[pos].set(
        jnp.arange(n_pairs, dtype=jnp.int32) // TOP_K)
    tile_blk = jnp.minimum(jnp.arange(n_tiles, dtype=jnp.int32), n_used[0] - 1)
    tile_e = jnp.minimum(jnp.sum(tile_blk[:, None] >= tile_end[None, :], axis=1), n_e - 1).astype(jnp.int32)
    return pos.reshape(-1, TOP_K).astype(jnp.int32), token_of, tile_e, tile_blk, n_used


def _row_copy(src_hbm, idx_ref, dst_ref, sem, r):
    return pltpu.make_async_copy(src_hbm.at[pl.ds(idx_ref[0, r], 1)], dst_ref.at[pl.ds(r, 1)], sem)


def _gather_rows(src_hbm, idx_ref, dst_ref, sem):
    n = dst_ref.shape[0]

    def start(r, c):
        _row_copy(src_hbm, idx_ref, dst_ref, sem, r).start()
        return c

    def wait(r, c):
        _row_copy(src_hbm, idx_ref, dst_ref, sem, r).wait()
        return c

    lax.fori_loop(0, n, start, 0)
    lax.fori_loop(0, n, wait, 0)


def _moe_gather_kernel(nu_ref, tok_ref, g_ref, x_hbm, o_ref, buf_ref, sem):
    used = pl.program_id(0) < nu_ref[0]

    @pl.when(used)
    def _():
        _gather_rows(x_hbm, tok_ref, buf_ref, sem.at[0])
        o_ref[...] = _rms(buf_ref[...], g_ref[...]).astype(o_ref.dtype)

    @pl.when(jnp.logical_not(used))
    def _():
        o_ref[...] = jnp.zeros_like(o_ref)


def _moe_gather(x, g3, layer, token_of, tile_blk, n_used, te_rows):
    d = x.shape[1]
    n_tiles = tile_blk.shape[0]
    tok3 = token_of.reshape(n_tiles, 1, te_rows)
    grid_spec = pltpu.PrefetchScalarGridSpec(
        num_scalar_prefetch=2,
        grid=(n_tiles,),
        in_specs=[pl.BlockSpec((None, 1, te_rows), lambda t, nu, tb: (tb[t], 0, 0),
                               memory_space=pltpu.SMEM),
                  pl.BlockSpec((None, 1, d), lambda t, nu, tb: (layer, 0, 0)),
                  pl.BlockSpec(memory_space=pl.ANY)],
        out_specs=pl.BlockSpec((te_rows, d), lambda t, nu, tb: (t, 0)),
        scratch_shapes=[pltpu.VMEM((te_rows, d), F32), pltpu.SemaphoreType.DMA((1,))],
    )

    def kern(nu_ref, tb_ref, tok_ref, g_ref, x_hbm, o_ref, buf_ref, sem):
        del tb_ref
        _moe_gather_kernel(nu_ref, tok_ref, g_ref, x_hbm, o_ref, buf_ref, sem)

    return pl.pallas_call(
        kern,
        grid_spec=grid_spec,
        out_shape=jax.ShapeDtypeStruct((n_tiles * te_rows, d), BF16),
        compiler_params=_params("arbitrary"),
        name="moe_gather",
    )(n_used, tile_blk, tok3, g3, x)


def _first_tile_of_expert(te_ref, t):
    return (t == 0) | (te_ref[t] != te_ref[jnp.maximum(t - 1, 0)])


def _moe_up_kernel(te_ref, tb_ref, nu_ref, x_ref, wg_ref, wu_ref, o_ref, wgbf_ref, wubf_ref):
    del tb_ref
    t = pl.program_id(1)
    used = t < nu_ref[0]

    @pl.when(used)
    def _():
        @pl.when(_first_tile_of_expert(te_ref, t))
        def _():
            wgbf_ref[...] = wg_ref[...].astype(BF16)
            wubf_ref[...] = wu_ref[...].astype(BF16)

        x = x_ref[...]
        g = jnp.dot(x, wgbf_ref[...], preferred_element_type=F32)
        u = jnp.dot(x, wubf_ref[...], preferred_element_type=F32)
        o_ref[...] = (jax.nn.silu(g) * u).astype(o_ref.dtype)

    @pl.when(jnp.logical_not(used))
    def _():
        o_ref[...] = jnp.zeros_like(o_ref)


def _moe_up(xs, wg, wu, jm, tile_e, tile_blk, n_used, te_rows):
    d = xs.shape[1]
    f = wg.shape[-1]
    n_tiles = tile_blk.shape[0]
    tf = _tile(f, COL_TILE, LANES)
    wspec = lambda: pl.BlockSpec((None, None, d, tf), lambda j, t, te, tb, nu: (jm, te[t], 0, j))
    grid_spec = pltpu.PrefetchScalarGridSpec(
        num_scalar_prefetch=3,
        grid=(f // tf, n_tiles),
        in_specs=[pl.BlockSpec((te_rows, d), lambda j, t, te, tb, nu: (tb[t], 0)), wspec(), wspec()],
        out_specs=pl.BlockSpec((te_rows, tf), lambda j, t, te, tb, nu: (t, j)),
        scratch_shapes=[pltpu.VMEM((d, tf), BF16), pltpu.VMEM((d, tf), BF16)],
    )
    return pl.pallas_call(
        _moe_up_kernel,
        grid_spec=grid_spec,
        out_shape=jax.ShapeDtypeStruct((xs.shape[0], f), BF16),
        compiler_params=_params("arbitrary", "arbitrary"),
        name="moe_gate_up",
    )(tile_e, tile_blk, n_used, xs, wg, wu)


def _moe_down_kernel(te_ref, tb_ref, nu_ref, h_ref, wd_ref, o_ref, wdbf_ref):
    del tb_ref
    t = pl.program_id(1)
    used = t < nu_ref[0]

    @pl.when(used)
    def _():
        @pl.when(_first_tile_of_expert(te_ref, t))
        def _():
            wdbf_ref[...] = wd_ref[...].astype(BF16)

        o_ref[...] = jnp.dot(h_ref[...], wdbf_ref[...], preferred_element_type=F32)

    @pl.when(jnp.logical_not(used))
    def _():
        o_ref[...] = jnp.zeros_like(o_ref)


def _moe_down(hs, wd, jm, tile_e, tile_blk, n_used, te_rows):
    f = hs.shape[1]
    d = wd.shape[-1]
    n_tiles = tile_blk.shape[0]
    tn = _tile(d, COL_TILE, LANES)
    grid_spec = pltpu.PrefetchScalarGridSpec(
        num_scalar_prefetch=3,
        grid=(d // tn, n_tiles),
        in_specs=[pl.BlockSpec((te_rows, f), lambda j, t, te, tb, nu: (tb[t], 0)),
                  pl.BlockSpec((None, None, f, tn), lambda j, t, te, tb, nu: (jm, te[t], 0, j))],
        out_specs=pl.BlockSpec((te_rows, tn), lambda j, t, te, tb, nu: (t, j)),
        scratch_shapes=[pltpu.VMEM((f, tn), BF16)],
    )
    return pl.pallas_call(
        _moe_down_kernel,
        grid_spec=grid_spec,
        out_shape=jax.ShapeDtypeStruct((hs.shape[0], d), F32),
        compiler_params=_params("arbitrary", "arbitrary"),
        name="moe_down",
    )(tile_e, tile_blk, n_used, hs, wd)


def _moe_combine_kernel(p1_ref, p2_ref, x_ref, wt_ref, ys_hbm, o_ref, b1_ref, b2_ref, sem):
    _gather_rows(ys_hbm, p1_ref, b1_ref, sem.at[0])
    _gather_rows(ys_hbm, p2_ref, b2_ref, sem.at[1])
    w = wt_ref[...]
    o_ref[...] = x_ref[...] + w[:, 0:1] * b1_ref[...] + w[:, 1:2] * b2_ref[...]


def _moe_combine(x, wt, ys, pos):
    m, d = x.shape
    tg = _tile(m, GATHER_TILE, 8)
    n_steps = m // tg
    p1 = pos[:, 0].reshape(n_steps, 1, tg)
    p2 = pos[:, 1].reshape(n_steps, 1, tg)
    pspec = lambda: pl.BlockSpec((None, 1, tg), lambda i: (i, 0, 0), memory_space=pltpu.SMEM)
    return pl.pallas_call(
        _moe_combine_kernel,
        grid=(n_steps,),
        in_specs=[pspec(), pspec(),
                  pl.BlockSpec((tg, d), lambda i: (i, 0)),
                  pl.BlockSpec((tg, LANES), lambda i: (i, 0)),
                  pl.BlockSpec(memory_space=pl.ANY)],
        out_specs=pl.BlockSpec((tg, d), lambda i: (i, 0)),
        out_shape=jax.ShapeDtypeStruct((m, d), F32),
        scratch_shapes=[pltpu.VMEM((tg, d), F32), pltpu.VMEM((tg, d), F32),
                        pltpu.SemaphoreType.DMA((2,))],
        compiler_params=_params("arbitrary"),
        name="moe_combine",
    )(p1, p2, x, wt, ys)


def kernel(x_prompt, x_sample, state_rglru_h, state_conv, norm_mix_g, w_in, conv_w, conv_b, lru_wa, lru_ba, lru_wx, lru_bx, lru_lambda, sgu_ln_g, sgu_ln_b, sgu_ws, sgu_bs, w_br_a, w_br_b, w_out, norm_ffn_g, ffn_wg, ffn_wu, ffn_wd, router_w, moe_wg, moe_wu, moe_wd, norm_final_g):
    batch, seq, d_model = x_prompt.shape
    bsz, n_t, _ = x_sample.shape
    depth = w_in.shape[0]
    d_rnn = conv_b.shape[-1]
    d_chunk = sgu_ln_g.shape[-1]
    rb = lru_wa.shape[-1]
    n_g = sgu_ws.shape[1]
    sg = d_chunk // n_g
    n_e = router_w.shape[-1]
    n_prompt = batch * seq
    n_sample = n_t * bsz
    m = n_prompt + n_sample
    assert n_prompt % n_sample == 0 and bsz % BF16_ROWS == 0 and n_e <= LANES
    sample_block = n_prompt // n_sample
    col_u, col_v, col_gate = d_rnn, d_rnn + d_chunk, d_rnn + 2 * d_chunk

    x = jnp.concatenate([x_prompt.reshape(n_prompt, d_model),
                         jnp.swapaxes(x_sample, 0, 1).reshape(n_sample, d_model)], axis=0)

    row3 = lambda a: a.reshape(a.shape[0], 1, -1)
    p = dict(
        conv_w=conv_w, conv_b=row3(conv_b), lru_wa=lru_wa, lru_wx=lru_wx,
        lru_ba=row3(lru_ba), lru_bx=row3(lru_bx), lru_lambda=row3(lru_lambda),
        sgu_ln_g=row3(sgu_ln_g), sgu_ln_b=row3(sgu_ln_b), sgu_ws=sgu_ws,
        sgu_bs_t=jnp.swapaxes(sgu_bs, 1, 2),
        sgu_coef=jnp.repeat(jnp.transpose(sgu_ws[:, :, :n_t, :n_t], (0, 2, 3, 1)).reshape(depth, n_t * n_t, n_g),
                            sg, axis=-1),
        sgu_bias=jnp.repeat(jnp.swapaxes(sgu_bs[:, :, :n_t], 1, 2), sg, axis=-1),
    )
    g_mix, g_ffn = row3(norm_mix_g), row3(norm_ffn_g)
    buf_t = jnp.swapaxes(state_conv, 1, 2)
    wr_pad = jnp.pad(router_w, ((0, 0), (0, 0), (0, LANES - n_e)))

    te_rows = _tile(m, ROW_TILE_HALF, BF16_ROWS)
    n_pairs = m * TOP_K
    n_tiles = (n_pairs + n_e * (te_rows - 1) + te_rows - 1) // te_rows
    tm_norm = _tile(m, ROW_TILE_HALF, BF16_ROWS)

    h_p, c_p, h_s, c_s, v_s = [], [], [], [], []
    for l in range(depth):
        xn = _rmsnorm(x, g_mix, l, tm_norm, BF16)
        z = _inproj(xn, w_in, l, d_rnn, d_chunk)

        ya_p, h_pl, c_pl = _rglru_prompt(z, p, l, batch, seq, rb)
        ya_s, h_sl, c_sl = _rglru_sample(z, buf_t, state_rglru_h, p, l, n_t, bsz, rb, sample_block)
        yb_p = _sgu_prompt(z, p, l, n_prompt, col_u, col_v)
        yb_s, v_sl = _sgu_sample(z, p, l, n_t, bsz, sample_block, col_u, col_v)
        h_p.append(h_pl.reshape(batch, d_rnn))
        c_p.append(c_pl)
        h_s.append(h_sl)
        c_s.append(jnp.swapaxes(c_sl, 0, 1))
        v_s.append(jnp.swapaxes(v_sl.reshape(n_t, bsz, d_chunk), 0, 1))

        mix = _merge(ya_p, yb_p, ya_s, yb_s, z, w_br_a, w_br_b, l, col_gate)
        x = _matmul_acc(mix, w_out, l, x)

        j = l // 2
        if l % 2 == 0:
            xn = _rmsnorm(x, g_ffn, l, tm_norm, BF16)
            hid = _gateup(xn, ffn_wg, ffn_wu, j)
            for kb in range(ffn_wd.shape[1] // min(K_CHUNK, ffn_wd.shape[1])):
                x = _matmul_acc(hid, ffn_wd, j, x, kb)
        else:
            idx, wt = _router(x, g_ffn, l, wr_pad, j, n_e)
            pos, token_of, tile_e, tile_blk, n_used = _route_meta(idx[:, :TOP_K], n_e, te_rows, n_tiles)
            xs = _moe_gather(x, g_ffn, l, token_of, tile_blk, n_used, te_rows)
            hs = _moe_up(xs, moe_wg, moe_wu, j, tile_e, tile_blk, n_used, te_rows)
            ys = _moe_down(hs, moe_wd, j, tile_e, tile_blk, n_used, te_rows)
            x = _moe_combine(x, wt, ys, pos)

    g_fin = norm_final_g.reshape(1, 1, d_model)
    tm_fin = _tile(n_sample, POS_TILE, 8)
    y_prompt = _rmsnorm(x, g_fin, 0, tm_fin, F32, 0, n_prompt).reshape(batch, seq, d_model)
    y_sample = _rmsnorm(x, g_fin, 0, tm_fin, F32, n_prompt // tm_fin, n_sample)
    y_sample = jnp.swapaxes(y_sample.reshape(n_t, bsz, d_model), 0, 1)
    return (y_prompt, y_sample, jnp.stack(h_p), jnp.stack(c_p), jnp.stack(h_s), jnp.stack(c_s),
            jnp.stack(v_s))
```

```python
import functools

import jax
import jax.numpy as jnp
from jax import lax
from jax.experimental import pallas as pl
from jax.experimental.pallas import tpu as pltpu

F32 = jnp.float32
BF16 = jnp.bfloat16

LRU_C = 8.0
EPS = 1e-6
PAST_LEN = 16384
TOP_K = 2

V7X_VMEM_BYTES = 64 * 1024 * 1024
VMEM_LIMIT_BYTES = V7X_VMEM_BYTES - 6 * 1024 * 1024
BF16_ROWS = 16
LANES = 128

ROW_TILE = 1088
ROW_TILE_HALF = 544
COL_TILE = 512
K_CHUNK = 4096
GATHER_TILE = 272
SCAN_CHUNK = 256
POS_TILE = 512
DMA_UNROLL = 8


def _params(*sem):
    return pltpu.CompilerParams(dimension_semantics=sem, vmem_limit_bytes=VMEM_LIMIT_BYTES)


def _tile(n, target, mult):
    for t in range(min(n, target), 0, -1):
        if n % t == 0 and t % mult == 0:
            return t
    raise ValueError(f"no tile for {n} (target {target}, multiple {mult})")


def _rms(x, g):
    return x * lax.rsqrt(jnp.mean(x * x, axis=-1, keepdims=True) + EPS) * g


def _sigmoid(x):
    return 0.5 * jnp.tanh(0.5 * x) + 0.5


def _norm_kernel(x_ref, g_ref, o_ref):
    o_ref[...] = _rms(x_ref[...], g_ref[...]).astype(o_ref.dtype)


def _rmsnorm(x, g3, layer, tm, out_dtype, row_block0=0, n_rows=None):
    m, d = x.shape
    n_rows = m if n_rows is None else n_rows
    return pl.pallas_call(
        _norm_kernel,
        grid=(n_rows // tm,),
        in_specs=[pl.BlockSpec((tm, d), lambda i: (i + row_block0, 0)),
                  pl.BlockSpec((None, 1, d), lambda i: (layer, 0, 0))],
        out_specs=pl.BlockSpec((tm, d), lambda i: (i, 0)),
        out_shape=jax.ShapeDtypeStruct((n_rows, d), out_dtype),
        compiler_params=_params("parallel"),
        name="rmsnorm",
    )(x, g3)


def _inproj_kernel(n_lin, n_gelu, x_ref, w_ref, o_ref, wbf_ref):
    j = pl.program_id(0)

    @pl.when(pl.program_id(1) == 0)
    def _():
        wbf_ref[...] = w_ref[...].astype(BF16)

    acc = jnp.dot(x_ref[...], wbf_ref[...], preferred_element_type=F32)

    @pl.when(j < n_lin)
    def _():
        o_ref[...] = acc.astype(o_ref.dtype)

    @pl.when((j >= n_lin) & (j < n_lin + n_gelu))
    def _():
        o_ref[...] = jax.nn.gelu(acc, approximate=True).astype(o_ref.dtype)

    @pl.when(j >= n_lin + n_gelu)
    def _():
        o_ref[...] = _sigmoid(acc).astype(o_ref.dtype)


def _inproj(xn, w_in, layer, d_rnn, d_chunk):
    m, d = xn.shape
    n = w_in.shape[-1]
    tm = _tile(m, ROW_TILE, BF16_ROWS)
    tn = _tile(d_chunk, COL_TILE, LANES)
    kern = functools.partial(_inproj_kernel, d_rnn // tn, 2 * d_chunk // tn)
    return pl.pallas_call(
        kern,
        grid=(n // tn, m // tm),
        in_specs=[pl.BlockSpec((tm, d), lambda j, i: (i, 0)),
                  pl.BlockSpec((None, d, tn), lambda j, i: (layer, 0, j))],
        out_specs=pl.BlockSpec((tm, tn), lambda j, i: (i, j)),
        out_shape=jax.ShapeDtypeStruct((m, n), F32),
        scratch_shapes=[pltpu.VMEM((d, tn), BF16)],
        compiler_params=_params("arbitrary", "arbitrary"),
        name="in_proj",
    )(xn, w_in)


def _merge_kernel(n_pt, yap_ref, ybp_ref, yas_ref, ybs_ref, wa_ref, wb_ref, sa_ref, sb_ref, o_ref,
                  wabf_ref, wbbf_ref):
    i = pl.program_id(1)

    @pl.when(i == 0)
    def _():
        wabf_ref[...] = wa_ref[...].astype(BF16)
        wbbf_ref[...] = wb_ref[...].astype(BF16)

    def tile(ya_ref, yb_ref):
        pa = jnp.dot(ya_ref[...], wabf_ref[...], preferred_element_type=F32)
        pb = jnp.dot(yb_ref[...], wbbf_ref[...], preferred_element_type=F32)
        o_ref[...] = (sa_ref[...].astype(F32) * pa + sb_ref[...].astype(F32) * pb).astype(o_ref.dtype)

    @pl.when(i < n_pt)
    def _():
        tile(yap_ref, ybp_ref)

    @pl.when(i >= n_pt)
    def _():
        tile(yas_ref, ybs_ref)


def _merge(ya_p, yb_p, ya_s, yb_s, z, w_br_a, w_br_b, layer, gate_col0):
    n_prompt, da = ya_p.shape
    tm = ya_s.shape[0]
    db = yb_p.shape[1]
    d = w_br_a.shape[-1]
    n_pt = n_prompt // tm
    tn = _tile(d, COL_TILE, LANES)
    ga0 = gate_col0 // tn
    gb0 = (gate_col0 + d) // tn
    prompt_rows = lambda j, i: (jnp.minimum(i, n_pt - 1), 0)
    once = pl.Buffered(1)
    return pl.pallas_call(
        functools.partial(_merge_kernel, n_pt),
        grid=(d // tn, n_pt + 1),
        in_specs=[pl.BlockSpec((tm, da), prompt_rows),
                  pl.BlockSpec((tm, db), prompt_rows),
                  pl.BlockSpec((tm, da), lambda j, i: (0, 0), pipeline_mode=once),
                  pl.BlockSpec((tm, db), lambda j, i: (0, 0), pipeline_mode=once),
                  pl.BlockSpec((None, da, tn), lambda j, i: (layer, 0, j)),
                  pl.BlockSpec((None, db, tn), lambda j, i: (layer, 0, j)),
                  pl.BlockSpec((tm, tn), lambda j, i: (i, ga0 + j)),
                  pl.BlockSpec((tm, tn), lambda j, i: (i, gb0 + j))],
        out_specs=pl.BlockSpec((tm, tn), lambda j, i: (i, j)),
        out_shape=jax.ShapeDtypeStruct((n_prompt + tm, d), BF16),
        scratch_shapes=[pltpu.VMEM((da, tn), BF16), pltpu.VMEM((db, tn), BF16)],
        compiler_params=_params("arbitrary", "arbitrary"),
        name="branch_merge",
    )(ya_p, yb_p, ya_s, yb_s, w_br_a, w_br_b, z, z)


def _matmul_acc_kernel(x_ref, w_ref, acc_ref, o_ref, wbf_ref):
    @pl.when(pl.program_id(1) == 0)
    def _():
        wbf_ref[...] = w_ref[...].astype(BF16)

    o_ref[...] = acc_ref[...] + jnp.dot(x_ref[...], wbf_ref[...], preferred_element_type=F32)


def _matmul_acc(x, w, layer, acc, k_block=0):
    m = x.shape[0]
    n = w.shape[-1]
    tk = min(K_CHUNK, w.shape[-2])
    tm = _tile(m, ROW_TILE, BF16_ROWS)
    tn = _tile(n, COL_TILE, LANES)
    return pl.pallas_call(
        _matmul_acc_kernel,
        grid=(n // tn, m // tm),
        in_specs=[pl.BlockSpec((tm, tk), lambda j, i: (i, k_block)),
                  pl.BlockSpec((None, tk, tn), lambda j, i: (layer, k_block, j)),
                  pl.BlockSpec((tm, tn), lambda j, i: (i, j))],
        out_specs=pl.BlockSpec((tm, tn), lambda j, i: (i, j)),
        out_shape=jax.ShapeDtypeStruct((m, n), F32),
        scratch_shapes=[pltpu.VMEM((tk, tn), BF16)],
        input_output_aliases={2: 0},
        compiler_params=_params("arbitrary", "arbitrary"),
        name="matmul_acc",
    )(x, w, acc)


def _gateup_kernel(x_ref, wg_ref, wu_ref, o_ref, wgbf_ref, wubf_ref):
    @pl.when(pl.program_id(1) == 0)
    def _():
        wgbf_ref[...] = wg_ref[...].astype(BF16)
        wubf_ref[...] = wu_ref[...].astype(BF16)

    x = x_ref[...]
    g = jnp.dot(x, wgbf_ref[...], preferred_element_type=F32)
    u = jnp.dot(x, wubf_ref[...], preferred_element_type=F32)
    o_ref[...] = (jax.nn.silu(g) * u).astype(o_ref.dtype)


def _gateup(xn, wg, wu, layer):
    m, d = xn.shape
    f = wg.shape[-1]
    tm = _tile(m, ROW_TILE_HALF, BF16_ROWS)
    tf = _tile(f, COL_TILE, LANES)
    return pl.pallas_call(
        _gateup_kernel,
        grid=(f // tf, m // tm),
        in_specs=[pl.BlockSpec((tm, d), lambda j, i: (i, 0)),
                  pl.BlockSpec((None, d, tf), lambda j, i: (layer, 0, j)),
                  pl.BlockSpec((None, d, tf), lambda j, i: (layer, 0, j))],
        out_specs=pl.BlockSpec((tm, tf), lambda j, i: (i, j)),
        out_shape=jax.ShapeDtypeStruct((m, f), BF16),
        scratch_shapes=[pltpu.VMEM((d, tf), BF16), pltpu.VMEM((d, tf), BF16)],
        compiler_params=_params("arbitrary", "arbitrary"),
        name="ffn_gate_up",
    )(xn, wg, wu)


def _lru_gates(xc, wa_bf, wx_bf, ba, bx, logsig):
    xb = xc.astype(BF16)
    r = _sigmoid(jnp.dot(xb, wa_bf, preferred_element_type=F32) + ba)
    i = _sigmoid(jnp.dot(xb, wx_bf, preferred_element_type=F32) + bx)
    log_a = LRU_C * r * logsig
    a = jnp.exp(log_a)
    one_minus_a2 = -jnp.tanh(log_a) * (a * a + 1.0)
    mult = jnp.where(one_minus_a2 > 0.0, one_minus_a2 * lax.rsqrt(one_minus_a2), 0.0)
    return a, mult, i * xc


def _scan8(a, b, h_prev, row):
    for d in (1, 2, 4):
        keep = row >= d
        a_s = jnp.where(keep, pltpu.roll(a, d, 0), 1.0)
        b_s = jnp.where(keep, pltpu.roll(b, d, 0), 0.0)
        b = b + a * b_s
        a = a * a_s
    y = b + a * h_prev
    return y, y[7:8, :]


def _rglru_prompt_kernel(conv_w, z_ref, cw_ref, cb_ref, wa_ref, wx_ref, ba_ref, bx_ref, lam_ref,
                         y_ref, h_ref, nb_ref, xp_ref, a_ref, b_ref):
    t_len, c = z_ref.shape
    pad = 8
    xp_ref[0:pad, :] = jnp.zeros((pad, c), F32)
    xp_ref[pad:pad + t_len, :] = z_ref[...].astype(F32)
    nb_ref[...] = xp_ref[pad + t_len - (conv_w - 1):pad + t_len, :]

    wa_bf = wa_ref[...].astype(BF16)
    wx_bf = wx_ref[...].astype(BF16)
    logsig = jax.nn.log_sigmoid(lam_ref[...])
    ba = ba_ref[...]
    bx = bx_ref[...]
    cb = cb_ref[...]
    row8 = lax.broadcasted_iota(jnp.int32, (8, c), 0)
    ch = min(SCAN_CHUNK, t_len)

    h = jnp.zeros((1, c), F32)
    for ci in range(t_len // ch):
        base = pad + ci * ch
        xc = cb
        for k in range(conv_w):
            off = base - (conv_w - 1) + k
            xc = xc + cw_ref[k:k + 1, :] * xp_ref[off:off + ch, :]
        a, mult, ix = _lru_gates(xc, wa_bf, wx_bf, ba, bx, logsig)
        if ci == 0:
            rows = lax.broadcasted_iota(jnp.int32, (ch, c), 0)
            mult = jnp.where(rows == 0, 1.0, mult)
        a_ref[...] = a
        b_ref[...] = mult * ix

        def group(g, h, ci=ci):
            r0 = pl.multiple_of(g * BF16_ROWS, BF16_ROWS)
            a16 = a_ref[pl.ds(r0, BF16_ROWS), :]
            b16 = b_ref[pl.ds(r0, BF16_ROWS), :]
            y0, h = _scan8(a16[0:8], b16[0:8], h, row8)
            y1, h = _scan8(a16[8:16], b16[8:16], h, row8)
            y_ref[pl.ds(ci * ch + r0, BF16_ROWS), :] = jnp.concatenate([y0, y1], 0).astype(y_ref.dtype)
            return h

        h = lax.fori_loop(0, ch // BF16_ROWS, group, h)
    h_ref[...] = h


def _rglru_prompt(z, p, layer, batch, seq, rb):
    d_rnn = p["conv_b"].shape[-1]
    conv_w = p["conv_w"].shape[1]
    nb = d_rnn // rb
    vec = lambda: pl.BlockSpec((None, 1, rb), lambda b, n: (layer, 0, n))
    blk = lambda: pl.BlockSpec((None, None, rb, rb), lambda b, n: (layer, n, 0, 0))
    ch = min(SCAN_CHUNK, seq)
    return pl.pallas_call(
        functools.partial(_rglru_prompt_kernel, conv_w),
        grid=(batch, nb),
        in_specs=[pl.BlockSpec((seq, rb), lambda b, n: (b, n)),
                  pl.BlockSpec((None, conv_w, rb), lambda b, n: (layer, 0, n)),
                  vec(), blk(), blk(), vec(), vec(), vec()],
        out_specs=[pl.BlockSpec((seq, rb), lambda b, n: (b, n)),
                   pl.BlockSpec((None, 1, rb), lambda b, n: (b, 0, n)),
                   pl.BlockSpec((None, conv_w - 1, rb), lambda b, n: (b, 0, n))],
        out_shape=[jax.ShapeDtypeStruct((batch * seq, d_rnn), BF16),
                   jax.ShapeDtypeStruct((batch, 1, d_rnn), F32),
                   jax.ShapeDtypeStruct((batch, conv_w - 1, d_rnn), F32)],
        scratch_shapes=[pltpu.VMEM((seq + 8, rb), F32),
                        pltpu.VMEM((ch, rb), F32), pltpu.VMEM((ch, rb), F32)],
        compiler_params=_params("parallel", "parallel"),
        name="rglru_prompt",
    )(z, p["conv_w"], p["conv_b"], p["lru_wa"], p["lru_wx"], p["lru_ba"], p["lru_bx"], p["lru_lambda"])


def _rglru_sample_kernel(conv_w, n_t, z_ref, buf_ref, h0_ref, cw_ref, cb_ref, wa_ref, wx_ref,
                         ba_ref, bx_ref, lam_ref, y_ref, h_ref, nb_ref):
    rows, c = z_ref.shape
    bsz = rows // n_t
    xr = z_ref[...].astype(F32)
    slabs = [buf_ref[k] for k in range(conv_w - 1)] + [xr[t * bsz:(t + 1) * bsz] for t in range(n_t)]
    cb = cb_ref[...]
    xcs = []
    for t in range(n_t):
        xc = cb
        for k in range(conv_w):
            xc = xc + cw_ref[k:k + 1, :] * slabs[t + k]
        xcs.append(xc)
    xc = jnp.concatenate(xcs, 0)
    a, mult, ix = _lru_gates(xc, wa_ref[...].astype(BF16), wx_ref[...].astype(BF16),
                             ba_ref[...], bx_ref[...], jax.nn.log_sigmoid(lam_ref[...]))
    b = mult * ix
    h = h0_ref[...]
    ys = []
    for t in range(n_t):
        h = a[t * bsz:(t + 1) * bsz] * h + b[t * bsz:(t + 1) * bsz]
        ys.append(h)
    y_ref[...] = jnp.concatenate(ys, 0).astype(y_ref.dtype)
    h_ref[...] = h
    for k in range(conv_w - 1):
        nb_ref[k] = slabs[n_t + k]


def _rglru_sample(z, buf_t, h0, p, layer, n_t, bsz, rb, row_block):
    d_rnn = p["conv_b"].shape[-1]
    conv_w = p["conv_w"].shape[1]
    assert PAST_LEN > 0 and n_t >= conv_w - 1
    rows = n_t * bsz
    vec = lambda: pl.BlockSpec((None, 1, rb), lambda n: (layer, 0, n))
    blk = lambda: pl.BlockSpec((None, None, rb, rb), lambda n: (layer, n, 0, 0))
    return pl.pallas_call(
        functools.partial(_rglru_sample_kernel, conv_w, n_t),
        grid=(d_rnn // rb,),
        in_specs=[pl.BlockSpec((rows, rb), lambda n: (row_block, n)),
                  pl.BlockSpec((None, conv_w - 1, bsz, rb), lambda n: (layer, 0, 0, n)),
                  pl.BlockSpec((None, bsz, rb), lambda n: (layer, 0, n)),
                  pl.BlockSpec((None, conv_w, rb), lambda n: (layer, 0, n)),
                  vec(), blk(), blk(), vec(), vec(), vec()],
        out_specs=[pl.BlockSpec((rows, rb), lambda n: (0, n)),
                   pl.BlockSpec((bsz, rb), lambda n: (0, n)),
                   pl.BlockSpec((conv_w - 1, bsz, rb), lambda n: (0, 0, n))],
        out_shape=[jax.ShapeDtypeStruct((rows, d_rnn), BF16),
                   jax.ShapeDtypeStruct((bsz, d_rnn), F32),
                   jax.ShapeDtypeStruct((conv_w - 1, bsz, d_rnn), F32)],
        compiler_params=_params("parallel"),
        name="rglru_sample",
    )(z, buf_t, h0, p["conv_w"], p["conv_b"], p["lru_wa"], p["lru_wx"], p["lru_ba"], p["lru_bx"],
      p["lru_lambda"])


def _layernorm(v, g, b):
    mu = jnp.mean(v, axis=-1, keepdims=True)
    var = jnp.mean(jnp.square(v - mu), axis=-1, keepdims=True)
    return (v - mu) * lax.rsqrt(var + EPS) * g + b


def _sgu_prompt_kernel(u_ref, v_ref, g_ref, b_ref, ws_ref, bst_ref, y_ref):
    rows = u_ref.shape[0]
    n_g, chunk, _ = ws_ref.shape
    sg = u_ref.shape[1] // n_g
    vn = _layernorm(v_ref[...].astype(F32), g_ref[...], b_ref[...]).astype(BF16)
    tri = (lax.broadcasted_iota(jnp.int32, (chunk, chunk), 0)
           >= lax.broadcasted_iota(jnp.int32, (chunk, chunk), 1))
    for g in range(n_g):
        w = jnp.where(tri, ws_ref[g], 0.0).astype(BF16)
        bias = bst_ref[:, g:g + 1]
        for ci in range(rows // chunk):
            rs = slice(ci * chunk, (ci + 1) * chunk)
            cs = slice(g * sg, (g + 1) * sg)
            s = jnp.dot(w, vn[rs, cs], preferred_element_type=F32) + bias
            y_ref[rs, cs] = (u_ref[rs, cs].astype(F32) * s).astype(y_ref.dtype)


def _sgu_prompt(z, p, layer, n_prompt, col_u, col_v):
    d_chunk = p["sgu_ln_g"].shape[-1]
    n_g, chunk = p["sgu_ws"].shape[1], p["sgu_ws"].shape[2]
    tm = _tile(n_prompt, POS_TILE, chunk)
    vec = lambda: pl.BlockSpec((None, 1, d_chunk), lambda i: (layer, 0, 0))
    return pl.pallas_call(
        _sgu_prompt_kernel,
        grid=(n_prompt // tm,),
        in_specs=[pl.BlockSpec((tm, d_chunk), lambda i: (i, col_u // d_chunk)),
                  pl.BlockSpec((tm, d_chunk), lambda i: (i, col_v // d_chunk)),
                  vec(), vec(),
                  pl.BlockSpec((None, n_g, chunk, chunk), lambda i: (layer, 0, 0, 0)),
                  pl.BlockSpec((None, chunk, n_g), lambda i: (layer, 0, 0))],
        out_specs=pl.BlockSpec((tm, d_chunk), lambda i: (i, 0)),
        out_shape=jax.ShapeDtypeStruct((n_prompt, d_chunk), BF16),
        compiler_params=_params("parallel"),
        name="sgu_prompt",
    )(z, z, p["sgu_ln_g"], p["sgu_ln_b"], p["sgu_ws"], p["sgu_bs_t"])


def _sgu_sample_kernel(n_t, u_ref, v_ref, g_ref, b_ref, coef_ref, bias_ref, y_ref, vn_ref):
    bsz = u_ref.shape[0] // n_t
    vn = _layernorm(v_ref[...].astype(F32), g_ref[...], b_ref[...])
    vn_ref[...] = vn
    vn = vn.astype(BF16).astype(F32)
    coef = coef_ref[...].astype(BF16).astype(F32)
    for t in range(n_t):
        s = bias_ref[t:t + 1, :]
        for k in range(t + 1):
            s = s + coef[t * n_t + k:t * n_t + k + 1, :] * vn[k * bsz:(k + 1) * bsz]
        rs = slice(t * bsz, (t + 1) * bsz)
        y_ref[rs, :] = (u_ref[rs, :].astype(F32) * s).astype(y_ref.dtype)


def _sgu_sample(z, p, layer, n_t, bsz, row_block, col_u, col_v):
    d_chunk = p["sgu_ln_g"].shape[-1]
    rows = n_t * bsz
    vec = lambda: pl.BlockSpec((None, 1, d_chunk), lambda i: (layer, 0, 0))
    return pl.pallas_call(
        functools.partial(_sgu_sample_kernel, n_t),
        grid=(1,),
        in_specs=[pl.BlockSpec((rows, d_chunk), lambda i: (row_block, col_u // d_chunk)),
                  pl.BlockSpec((rows, d_chunk), lambda i: (row_block, col_v // d_chunk)),
                  vec(), vec(),
                  pl.BlockSpec((None, n_t * n_t, d_chunk), lambda i: (layer, 0, 0)),
                  pl.BlockSpec((None, n_t, d_chunk), lambda i: (layer, 0, 0))],
        out_specs=[pl.BlockSpec((rows, d_chunk), lambda i: (0, 0)),
                   pl.BlockSpec((rows, d_chunk), lambda i: (0, 0))],
        out_shape=[jax.ShapeDtypeStruct((rows, d_chunk), BF16),
                   jax.ShapeDtypeStruct((rows, d_chunk), F32)],
        compiler_params=_params("arbitrary"),
        name="sgu_sample",
    )(z, z, p["sgu_ln_g"], p["sgu_ln_b"], p["sgu_coef"], p["sgu_bias"])


def _router_kernel(n_e, x_ref, g_ref, wr_ref, idx_ref, wt_ref):
    xn = _rms(x_ref[...], g_ref[...])
    logits = jnp.dot(xn.astype(BF16), wr_ref[...].astype(BF16), preferred_element_type=F32)
    lane = lax.broadcasted_iota(jnp.int32, logits.shape, 1)
    neg = jnp.float32(-jnp.inf)
    l1 = jnp.where(lane < n_e, logits, neg)
    m1 = jnp.max(l1, axis=-1, keepdims=True)
    i1 = jnp.min(jnp.where(l1 == m1, lane, LANES), axis=-1, keepdims=True)
    l2 = jnp.where(lane == i1, neg, l1)
    m2 = jnp.max(l2, axis=-1, keepdims=True)
    i2 = jnp.min(jnp.where(l2 == m2, lane, LANES), axis=-1, keepdims=True)
    e2 = jnp.exp(m2 - m1)
    w1 = 1.0 / (1.0 + e2)
    w2 = e2 / (1.0 + e2)
    idx_ref[...] = jnp.where(lane == 0, i1, jnp.where(lane == 1, i2, 0))
    wt_ref[...] = jnp.where(lane == 0, w1, jnp.where(lane == 1, w2, 0.0))


def _router(x, g3, layer, wr_pad, jm, n_e):
    m, d = x.shape
    tm = _tile(m, ROW_TILE_HALF, 8)
    return pl.pallas_call(
        functools.partial(_router_kernel, n_e),
        grid=(m // tm,),
        in_specs=[pl.BlockSpec((tm, d), lambda i: (i, 0)),
                  pl.BlockSpec((None, 1, d), lambda i: (layer, 0, 0)),
                  pl.BlockSpec((None, d, LANES), lambda i: (jm, 0, 0))],
        out_specs=[pl.BlockSpec((tm, LANES), lambda i: (i, 0)),
                   pl.BlockSpec((tm, LANES), lambda i: (i, 0))],
        out_shape=[jax.ShapeDtypeStruct((m, LANES), jnp.int32),
                   jax.ShapeDtypeStruct((m, LANES), F32)],
        compiler_params=_params("parallel"),
        name="moe_router",
    )(x, g3, wr_pad)


def _route_meta(idx2, n_e, te_rows, n_tiles):
    e = idx2.reshape(-1)
    n_pairs = e.shape[0]
    onehot = (e[:, None] == jnp.arange(n_e, dtype=jnp.int32)[None, :]).astype(jnp.int32)
    rank = jnp.sum((jnp.cumsum(onehot, axis=0) - onehot) * onehot, axis=1)
    count = jnp.sum(onehot, axis=0)
    tiles_e = (count + te_rows - 1) // te_rows
    tile_end = jnp.cumsum(tiles_e)
    tile_start = tile_end - tiles_e
    pos = tile_start[e] * te_rows + rank
    n_used = tile_end[-1:].astype(jnp.int32)
    token_of = jnp.zeros((n_tiles * te_rows,), jnp.int32).at[pos].set(
        jnp.arange(n_pairs, dtype=jnp.int32) // TOP_K)
    tile_blk = jnp.minimum(jnp.arange(n_tiles, dtype=jnp.int32), n_used[0] - 1)
    tile_e = jnp.minimum(jnp.sum(tile_blk[:, None] >= tile_end[None, :], axis=1), n_e - 1).astype(jnp.int32)
    return pos.reshape(-1, TOP_K).astype(jnp.int32), token_of, tile_e, tile_blk, n_used


def _row_copy(src_hbm, idx_ref, dst_ref, sem, r):
    return pltpu.make_async_copy(src_hbm.at[pl.ds(idx_ref[0, r], 1)], dst_ref.at[pl.ds(r, 1)], sem)


def _start_rows(src_hbm, idx_ref, dst_ref, sem):
    def start(r, c):
        _row_copy(src_hbm, idx_ref, dst_ref, sem, r).start()
        return c

    lax.fori_loop(0, dst_ref.shape[0], start, 0, unroll=DMA_UNROLL)


def _wait_rows(src_hbm, idx_ref, dst_ref, sem):
    def wait(r, c):
        _row_copy(src_hbm, idx_ref, dst_ref, sem, r).wait()
        return c

    lax.fori_loop(0, dst_ref.shape[0], wait, 0, unroll=DMA_UNROLL)


def _moe_gather_kernel(nu_ref, tok_ref, tok_next_ref, g_ref, x_hbm, o_ref, buf_ref, sem):
    t = pl.program_id(0)
    n_used = nu_ref[0]
    slot = t % 2

    @pl.when(t == 0)
    def _():
        _start_rows(x_hbm, tok_ref, buf_ref.at[0], sem.at[0])

    @pl.when(t + 1 < n_used)
    def _():
        _start_rows(x_hbm, tok_next_ref, buf_ref.at[1 - slot], sem.at[1 - slot])

    @pl.when(t < n_used)
    def _():
        _wait_rows(x_hbm, tok_ref, buf_ref.at[slot], sem.at[slot])
        o_ref[...] = _rms(buf_ref[slot], g_ref[...]).astype(o_ref.dtype)

    @pl.when(t >= n_used)
    def _():
        o_ref[...] = jnp.zeros_like(o_ref)


def _moe_gather(x, g3, layer, token_of, tile_blk, n_used, te_rows):
    d = x.shape[1]
    n_tiles = tile_blk.shape[0]
    tok3 = token_of.reshape(n_tiles, 1, te_rows)
    tok_spec = lambda step: pl.BlockSpec(
        (None, 1, te_rows), lambda t, nu, tb: (tb[jnp.minimum(t + step, n_tiles - 1)], 0, 0),
        memory_space=pltpu.SMEM)
    grid_spec = pltpu.PrefetchScalarGridSpec(
        num_scalar_prefetch=2,
        grid=(n_tiles,),
        in_specs=[tok_spec(0), tok_spec(1),
                  pl.BlockSpec((None, 1, d), lambda t, nu, tb: (layer, 0, 0)),
                  pl.BlockSpec(memory_space=pl.ANY)],
        out_specs=pl.BlockSpec((te_rows, d), lambda t, nu, tb: (t, 0)),
        scratch_shapes=[pltpu.VMEM((2, te_rows, d), F32), pltpu.SemaphoreType.DMA((2,))],
    )

    def kern(nu_ref, tb_ref, tok_ref, tok_next_ref, g_ref, x_hbm, o_ref, buf_ref, sem):
        del tb_ref
        _moe_gather_kernel(nu_ref, tok_ref, tok_next_ref, g_ref, x_hbm, o_ref, buf_ref, sem)

    return pl.pallas_call(
        kern,
        grid_spec=grid_spec,
        out_shape=jax.ShapeDtypeStruct((n_tiles * te_rows, d), BF16),
        compiler_params=_params("arbitrary"),
        name="moe_gather",
    )(n_used, tile_blk, tok3, tok3, g3, x)


def _first_tile_of_expert(te_ref, t):
    return (t == 0) | (te_ref[t] != te_ref[jnp.maximum(t - 1, 0)])


def _moe_up_kernel(te_ref, tb_ref, nu_ref, x_ref, wg_ref, wu_ref, o_ref, wgbf_ref, wubf_ref):
    del tb_ref
    t = pl.program_id(1)
    used = t < nu_ref[0]

    @pl.when(used)
    def _():
        @pl.when(_first_tile_of_expert(te_ref, t))
        def _():
            wgbf_ref[...] = wg_ref[...].astype(BF16)
            wubf_ref[...] = wu_ref[...].astype(BF16)

        x = x_ref[...]
        g = jnp.dot(x, wgbf_ref[...], preferred_element_type=F32)
        u = jnp.dot(x, wubf_ref[...], preferred_element_type=F32)
        o_ref[...] = (jax.nn.silu(g) * u).astype(o_ref.dtype)

    @pl.when(jnp.logical_not(used))
    def _():
        o_ref[...] = jnp.zeros_like(o_ref)


def _moe_up(xs, wg, wu, jm, tile_e, tile_blk, n_used, te_rows):
    d = xs.shape[1]
    f = wg.shape[-1]
    n_tiles = tile_blk.shape[0]
    tf = _tile(f, COL_TILE, LANES)
    wspec = lambda: pl.BlockSpec((None, None, d, tf), lambda j, t, te, tb, nu: (jm, te[t], 0, j))
    grid_spec = pltpu.PrefetchScalarGridSpec(
        num_scalar_prefetch=3,
        grid=(f // tf, n_tiles),
        in_specs=[pl.BlockSpec((te_rows, d), lambda j, t, te, tb, nu: (tb[t], 0)), wspec(), wspec()],
        out_specs=pl.BlockSpec((te_rows, tf), lambda j, t, te, tb, nu: (t, j)),
        scratch_shapes=[pltpu.VMEM((d, tf), BF16), pltpu.VMEM((d, tf), BF16)],
    )
    return pl.pallas_call(
        _moe_up_kernel,
        grid_spec=grid_spec,
        out_shape=jax.ShapeDtypeStruct((xs.shape[0], f), BF16),
        compiler_params=_params("arbitrary", "arbitrary"),
        name="moe_gate_up",
    )(tile_e, tile_blk, n_used, xs, wg, wu)


def _moe_down_kernel(te_ref, tb_ref, nu_ref, h_ref, wd_ref, o_ref, wdbf_ref):
    del tb_ref
    t = pl.program_id(1)
    used = t < nu_ref[0]

    @pl.when(used)
    def _():
        @pl.when(_first_tile_of_expert(te_ref, t))
        def _():
            wdbf_ref[...] = wd_ref[...].astype(BF16)

        o_ref[...] = jnp.dot(h_ref[...], wdbf_ref[...], preferred_element_type=F32)

    @pl.when(jnp.logical_not(used))
    def _():
        o_ref[...] = jnp.zeros_like(o_ref)


def _moe_down(hs, wd, jm, tile_e, tile_blk, n_used, te_rows):
    f = hs.shape[1]
    d = wd.shape[-1]
    n_tiles = tile_blk.shape[0]
    tn = _tile(d, COL_TILE, LANES)
    grid_spec = pltpu.PrefetchScalarGridSpec(
        num_scalar_prefetch=3,
        grid=(d // tn, n_tiles),
        in_specs=[pl.BlockSpec((te_rows, f), lambda j, t, te, tb, nu: (tb[t], 0)),
                  pl.BlockSpec((None, None, f, tn), lambda j, t, te, tb, nu: (jm, te[t], 0, j))],
        out_specs=pl.BlockSpec((te_rows, tn), lambda j, t, te, tb, nu: (t, j)),
        scratch_shapes=[pltpu.VMEM((f, tn), BF16)],
    )
    return pl.pallas_call(
        _moe_down_kernel,
        grid_spec=grid_spec,
        out_shape=jax.ShapeDtypeStruct((hs.shape[0], d), F32),
        compiler_params=_params("arbitrary", "arbitrary"),
        name="moe_down",
    )(tile_e, tile_blk, n_used, hs, wd)


def _moe_combine_kernel(p1_ref, p2_ref, p1_next_ref, p2_next_ref, x_ref, wt_ref, ys_hbm, o_ref,
                        b1_ref, b2_ref, sem):
    i = pl.program_id(0)
    slot = i % 2

    @pl.when(i == 0)
    def _():
        _start_rows(ys_hbm, p1_ref, b1_ref.at[0], sem.at[0, 0])
        _start_rows(ys_hbm, p2_ref, b2_ref.at[0], sem.at[1, 0])

    @pl.when(i + 1 < pl.num_programs(0))
    def _():
        _start_rows(ys_hbm, p1_next_ref, b1_ref.at[1 - slot], sem.at[0, 1 - slot])
        _start_rows(ys_hbm, p2_next_ref, b2_ref.at[1 - slot], sem.at[1, 1 - slot])

    _wait_rows(ys_hbm, p1_ref, b1_ref.at[slot], sem.at[0, slot])
    _wait_rows(ys_hbm, p2_ref, b2_ref.at[slot], sem.at[1, slot])
    w = wt_ref[...]
    o_ref[...] = x_ref[...] + w[:, 0:1] * b1_ref[slot] + w[:, 1:2] * b2_ref[slot]


def _moe_combine(x, wt, ys, pos):
    m, d = x.shape
    tg = _tile(m, GATHER_TILE, 8)
    n_steps = m // tg
    p1 = pos[:, 0].reshape(n_steps, 1, tg)
    p2 = pos[:, 1].reshape(n_steps, 1, tg)
    pspec = lambda step: pl.BlockSpec((None, 1, tg), lambda i: (jnp.minimum(i + step, n_steps - 1), 0, 0),
                                      memory_space=pltpu.SMEM)
    return pl.pallas_call(
        _moe_combine_kernel,
        grid=(n_steps,),
        in_specs=[pspec(0), pspec(0), pspec(1), pspec(1),
                  pl.BlockSpec((tg, d), lambda i: (i, 0)),
                  pl.BlockSpec((tg, LANES), lambda i: (i, 0)),
                  pl.BlockSpec(memory_space=pl.ANY)],
        out_specs=pl.BlockSpec((tg, d), lambda i: (i, 0)),
        out_shape=jax.ShapeDtypeStruct((m, d), F32),
        scratch_shapes=[pltpu.VMEM((2, tg, d), F32), pltpu.VMEM((2, tg, d), F32),
                        pltpu.SemaphoreType.DMA((2, 2))],
        compiler_params=_params("arbitrary"),
        name="moe_combine",
    )(p1, p2, p1, p2, x, wt, ys)


def kernel(x_prompt, x_sample, state_rglru_h, state_conv, norm_mix_g, w_in, conv_w, conv_b, lru_wa, lru_ba, lru_wx, lru_bx, lru_lambda, sgu_ln_g, sgu_ln_b, sgu_ws, sgu_bs, w_br_a, w_br_b, w_out, norm_ffn_g, ffn_wg, ffn_wu, ffn_wd, router_w, moe_wg, moe_wu, moe_wd, norm_final_g):
    batch, seq, d_model = x_prompt.shape
    bsz, n_t, _ = x_sample.shape
    depth = w_in.shape[0]
    d_rnn = conv_b.shape[-1]
    d_chunk = sgu_ln_g.shape[-1]
    rb = lru_wa.shape[-1]
    n_g = sgu_ws.shape[1]
    sg = d_chunk // n_g
    n_e = router_w.shape[-1]
    n_prompt = batch * seq
    n_sample = n_t * bsz
    m = n_prompt + n_sample
    assert n_prompt % n_sample == 0 and bsz % BF16_ROWS == 0 and n_e <= LANES
    sample_block = n_prompt // n_sample
    col_u, col_v, col_gate = d_rnn, d_rnn + d_chunk, d_rnn + 2 * d_chunk

    x = jnp.concatenate([x_prompt.reshape(n_prompt, d_model),
                         jnp.swapaxes(x_sample, 0, 1).reshape(n_sample, d_model)], axis=0)

    row3 = lambda a: a.reshape(a.shape[0], 1, -1)
    p = dict(
        conv_w=conv_w, conv_b=row3(conv_b), lru_wa=lru_wa, lru_wx=lru_wx,
        lru_ba=row3(lru_ba), lru_bx=row3(lru_bx), lru_lambda=row3(lru_lambda),
        sgu_ln_g=row3(sgu_ln_g), sgu_ln_b=row3(sgu_ln_b), sgu_ws=sgu_ws,
        sgu_bs_t=jnp.swapaxes(sgu_bs, 1, 2),
        sgu_coef=jnp.repeat(jnp.transpose(sgu_ws[:, :, :n_t, :n_t], (0, 2, 3, 1)).reshape(depth, n_t * n_t, n_g),
                            sg, axis=-1),
        sgu_bias=jnp.repeat(jnp.swapaxes(sgu_bs[:, :, :n_t], 1, 2), sg, axis=-1),
    )
    g_mix, g_ffn = row3(norm_mix_g), row3(norm_ffn_g)
    buf_t = jnp.swapaxes(state_conv, 1, 2)
    wr_pad = jnp.pad(router_w, ((0, 0), (0, 0), (0, LANES - n_e)))

    te_rows = _tile(m, ROW_TILE_HALF, BF16_ROWS)
    n_pairs = m * TOP_K
    n_tiles = (n_pairs + n_e * (te_rows - 1) + te_rows - 1) // te_rows
    tm_norm = _tile(m, ROW_TILE_HALF, BF16_ROWS)

    h_p, c_p, h_s, c_s, v_s = [], [], [], [], []
    for l in range(depth):
        xn = _rmsnorm(x, g_mix, l, tm_norm, BF16)
        z = _inproj(xn, w_in, l, d_rnn, d_chunk)

        ya_p, h_pl, c_pl = _rglru_prompt(z, p, l, batch, seq, rb)
        ya_s, h_sl, c_sl = _rglru_sample(z, buf_t, state_rglru_h, p, l, n_t, bsz, rb, sample_block)
        yb_p = _sgu_prompt(z, p, l, n_prompt, col_u, col_v)
        yb_s, v_sl = _sgu_sample(z, p, l, n_t, bsz, sample_block, col_u, col_v)
        h_p.append(h_pl.reshape(batch, d_rnn))
        c_p.append(c_pl)
        h_s.append(h_sl)
        c_s.append(jnp.swapaxes(c_sl, 0, 1))
        v_s.append(jnp.swapaxes(v_sl.reshape(n_t, bsz, d_chunk), 0, 1))

        mix = _merge(ya_p, yb_p, ya_s, yb_s, z, w_br_a, w_br_b, l, col_gate)
        x = _matmul_acc(mix, w_out, l, x)

        j = l // 2
        if l % 2 == 0:
            xn = _rmsnorm(x, g_ffn, l, tm_norm, BF16)
            hid = _gateup(xn, ffn_wg, ffn_wu, j)
            for kb in range(ffn_wd.shape[1] // min(K_CHUNK, ffn_wd.shape[1])):
                x = _matmul_acc(hid, ffn_wd, j, x, kb)
        else:
            idx, wt = _router(x, g_ffn, l, wr_pad, j, n_e)
            pos, token_of, tile_e, tile_blk, n_used = _route_meta(idx[:, :TOP_K], n_e, te_rows, n_tiles)
            xs = _moe_gather(x, g_ffn, l, token_of, tile_blk, n_used, te_rows)
            hs = _moe_up(xs, moe_wg, moe_wu, j, tile_e, tile_blk, n_used, te_rows)
            ys = _moe_down(hs, moe_wd, j, tile_e, tile_blk, n_used, te_rows)
            x = _moe_combine(x, wt, ys, pos)

    g_fin = norm_final_g.reshape(1, 1, d_model)
    tm_fin = _tile(n_sample, POS_TILE, 8)
    y_prompt = _rmsnorm(x, g_fin, 0, tm_fin, F32, 0, n_prompt).reshape(batch, seq, d_model)
    y_sample = _rmsnorm(x, g_fin, 0, tm_fin, F32, n_prompt // tm_fin, n_sample)
    y_sample = jnp.swapaxes(y_sample.reshape(n_t, bsz, d_model), 0, 1)
    return (y_prompt, y_sample, jnp.stack(h_p), jnp.stack(c_p), jnp.stack(h_s), jnp.stack(c_s),
            jnp.stack(v_s))
```

```python
import functools

import jax
import jax.numpy as jnp
from jax import lax
from jax.experimental import pallas as pl
from jax.experimental.pallas import tpu as pltpu

F32 = jnp.float32
BF16 = jnp.bfloat16

LRU_C = 8.0
EPS = 1e-6
PAST_LEN = 16384
TOP_K = 2

V7X_VMEM_BYTES = 64 * 1024 * 1024
VMEM_LIMIT_BYTES = V7X_VMEM_BYTES - 6 * 1024 * 1024
BF16_ROWS = 16
LANES = 128

ROW_TILE = 1088
ROW_TILE_HALF = 544
COL_TILE = 512
K_CHUNK = 4096
GATHER_TILE = 272
SCAN_CHUNK = 256
POS_TILE = 512
DMA_UNROLL = 8
DMA_PRIORITIES = 2


def _params(*sem):
    return pltpu.CompilerParams(dimension_semantics=sem, vmem_limit_bytes=VMEM_LIMIT_BYTES)


def _tile(n, target, mult):
    for t in range(min(n, target), 0, -1):
        if n % t == 0 and t % mult == 0:
            return t
    raise ValueError(f"no tile for {n} (target {target}, multiple {mult})")


def _rms(x, g):
    return x * lax.rsqrt(jnp.mean(x * x, axis=-1, keepdims=True) + EPS) * g


def _sigmoid(x):
    return 0.5 * jnp.tanh(0.5 * x) + 0.5


def _norm_kernel(x_ref, g_ref, o_ref):
    o_ref[...] = _rms(x_ref[...], g_ref[...]).astype(o_ref.dtype)


def _rmsnorm(x, g3, layer, tm, out_dtype, row_block0=0, n_rows=None):
    m, d = x.shape
    n_rows = m if n_rows is None else n_rows
    return pl.pallas_call(
        _norm_kernel,
        grid=(n_rows // tm,),
        in_specs=[pl.BlockSpec((tm, d), lambda i: (i + row_block0, 0)),
                  pl.BlockSpec((None, 1, d), lambda i: (layer, 0, 0))],
        out_specs=pl.BlockSpec((tm, d), lambda i: (i, 0)),
        out_shape=jax.ShapeDtypeStruct((n_rows, d), out_dtype),
        compiler_params=_params("parallel"),
        name="rmsnorm",
    )(x, g3)


def _inproj_kernel(n_lin, n_gelu, x_ref, w_ref, o_ref, wbf_ref):
    j = pl.program_id(0)

    @pl.when(pl.program_id(1) == 0)
    def _():
        wbf_ref[...] = w_ref[...].astype(BF16)

    acc = jnp.dot(x_ref[...], wbf_ref[...], preferred_element_type=F32)

    @pl.when(j < n_lin)
    def _():
        o_ref[...] = acc.astype(o_ref.dtype)

    @pl.when((j >= n_lin) & (j < n_lin + n_gelu))
    def _():
        o_ref[...] = jax.nn.gelu(acc, approximate=True).astype(o_ref.dtype)

    @pl.when(j >= n_lin + n_gelu)
    def _():
        o_ref[...] = _sigmoid(acc).astype(o_ref.dtype)


def _inproj(xn, w_in, layer, d_rnn, d_chunk):
    m, d = xn.shape
    n = w_in.shape[-1]
    tm = _tile(m, ROW_TILE, BF16_ROWS)
    tn = _tile(d_chunk, COL_TILE, LANES)
    kern = functools.partial(_inproj_kernel, d_rnn // tn, 2 * d_chunk // tn)
    return pl.pallas_call(
        kern,
        grid=(n // tn, m // tm),
        in_specs=[pl.BlockSpec((tm, d), lambda j, i: (i, 0)),
                  pl.BlockSpec((None, d, tn), lambda j, i: (layer, 0, j))],
        out_specs=pl.BlockSpec((tm, tn), lambda j, i: (i, j)),
        out_shape=jax.ShapeDtypeStruct((m, n), F32),
        scratch_shapes=[pltpu.VMEM((d, tn), BF16)],
        compiler_params=_params("arbitrary", "arbitrary"),
        name="in_proj",
    )(xn, w_in)


def _merge_kernel(n_pt, yap_ref, ybp_ref, yas_ref, ybs_ref, wa_ref, wb_ref, sa_ref, sb_ref, o_ref,
                  wabf_ref, wbbf_ref):
    i = pl.program_id(1)

    @pl.when(i == 0)
    def _():
        wabf_ref[...] = wa_ref[...].astype(BF16)
        wbbf_ref[...] = wb_ref[...].astype(BF16)

    def tile(ya_ref, yb_ref):
        pa = jnp.dot(ya_ref[...], wabf_ref[...], preferred_element_type=F32)
        pb = jnp.dot(yb_ref[...], wbbf_ref[...], preferred_element_type=F32)
        o_ref[...] = (sa_ref[...].astype(F32) * pa + sb_ref[...].astype(F32) * pb).astype(o_ref.dtype)

    @pl.when(i < n_pt)
    def _():
        tile(yap_ref, ybp_ref)

    @pl.when(i >= n_pt)
    def _():
        tile(yas_ref, ybs_ref)


def _merge(ya_p, yb_p, ya_s, yb_s, z, w_br_a, w_br_b, layer, gate_col0):
    n_prompt, da = ya_p.shape
    tm = ya_s.shape[0]
    db = yb_p.shape[1]
    d = w_br_a.shape[-1]
    n_pt = n_prompt // tm
    tn = _tile(d, COL_TILE, LANES)
    ga0 = gate_col0 // tn
    gb0 = (gate_col0 + d) // tn
    prompt_rows = lambda j, i: (jnp.minimum(i, n_pt - 1), 0)
    once = pl.Buffered(1)
    return pl.pallas_call(
        functools.partial(_merge_kernel, n_pt),
        grid=(d // tn, n_pt + 1),
        in_specs=[pl.BlockSpec((tm, da), prompt_rows),
                  pl.BlockSpec((tm, db), prompt_rows),
                  pl.BlockSpec((tm, da), lambda j, i: (0, 0), pipeline_mode=once),
                  pl.BlockSpec((tm, db), lambda j, i: (0, 0), pipeline_mode=once),
                  pl.BlockSpec((None, da, tn), lambda j, i: (layer, 0, j)),
                  pl.BlockSpec((None, db, tn), lambda j, i: (layer, 0, j)),
                  pl.BlockSpec((tm, tn), lambda j, i: (i, ga0 + j)),
                  pl.BlockSpec((tm, tn), lambda j, i: (i, gb0 + j))],
        out_specs=pl.BlockSpec((tm, tn), lambda j, i: (i, j)),
        out_shape=jax.ShapeDtypeStruct((n_prompt + tm, d), BF16),
        scratch_shapes=[pltpu.VMEM((da, tn), BF16), pltpu.VMEM((db, tn), BF16)],
        compiler_params=_params("arbitrary", "arbitrary"),
        name="branch_merge",
    )(ya_p, yb_p, ya_s, yb_s, w_br_a, w_br_b, z, z)


def _matmul_acc_kernel(x_ref, w_ref, acc_ref, o_ref, wbf_ref):
    @pl.when(pl.program_id(1) == 0)
    def _():
        wbf_ref[...] = w_ref[...].astype(BF16)

    o_ref[...] = acc_ref[...] + jnp.dot(x_ref[...], wbf_ref[...], preferred_element_type=F32)


def _matmul_acc(x, w, layer, acc, k_block=0):
    m = x.shape[0]
    n = w.shape[-1]
    tk = min(K_CHUNK, w.shape[-2])
    tm = _tile(m, ROW_TILE, BF16_ROWS)
    tn = _tile(n, COL_TILE, LANES)
    return pl.pallas_call(
        _matmul_acc_kernel,
        grid=(n // tn, m // tm),
        in_specs=[pl.BlockSpec((tm, tk), lambda j, i: (i, k_block)),
                  pl.BlockSpec((None, tk, tn), lambda j, i: (layer, k_block, j)),
                  pl.BlockSpec((tm, tn), lambda j, i: (i, j))],
        out_specs=pl.BlockSpec((tm, tn), lambda j, i: (i, j)),
        out_shape=jax.ShapeDtypeStruct((m, n), F32),
        scratch_shapes=[pltpu.VMEM((tk, tn), BF16)],
        input_output_aliases={2: 0},
        compiler_params=_params("arbitrary", "arbitrary"),
        name="matmul_acc",
    )(x, w, acc)


def _gateup_kernel(x_ref, wg_ref, wu_ref, o_ref, wgbf_ref, wubf_ref):
    @pl.when(pl.program_id(1) == 0)
    def _():
        wgbf_ref[...] = wg_ref[...].astype(BF16)
        wubf_ref[...] = wu_ref[...].astype(BF16)

    x = x_ref[...]
    g = jnp.dot(x, wgbf_ref[...], preferred_element_type=F32)
    u = jnp.dot(x, wubf_ref[...], preferred_element_type=F32)
    o_ref[...] = (jax.nn.silu(g) * u).astype(o_ref.dtype)


def _gateup(xn, wg, wu, layer):
    m, d = xn.shape
    f = wg.shape[-1]
    tm = _tile(m, ROW_TILE_HALF, BF16_ROWS)
    tf = _tile(f, COL_TILE, LANES)
    return pl.pallas_call(
        _gateup_kernel,
        grid=(f // tf, m // tm),
        in_specs=[pl.BlockSpec((tm, d), lambda j, i: (i, 0)),
                  pl.BlockSpec((None, d, tf), lambda j, i: (layer, 0, j)),
                  pl.BlockSpec((None, d, tf), lambda j, i: (layer, 0, j))],
        out_specs=pl.BlockSpec((tm, tf), lambda j, i: (i, j)),
        out_shape=jax.ShapeDtypeStruct((m, f), BF16),
        scratch_shapes=[pltpu.VMEM((d, tf), BF16), pltpu.VMEM((d, tf), BF16)],
        compiler_params=_params("arbitrary", "arbitrary"),
        name="ffn_gate_up",
    )(xn, wg, wu)


def _lru_gates(xc, wa_bf, wx_bf, ba, bx, logsig):
    xb = xc.astype(BF16)
    r = _sigmoid(jnp.dot(xb, wa_bf, preferred_element_type=F32) + ba)
    i = _sigmoid(jnp.dot(xb, wx_bf, preferred_element_type=F32) + bx)
    log_a = LRU_C * r * logsig
    a = jnp.exp(log_a)
    one_minus_a2 = -jnp.tanh(log_a) * (a * a + 1.0)
    mult = jnp.where(one_minus_a2 > 0.0, one_minus_a2 * lax.rsqrt(one_minus_a2), 0.0)
    return a, mult, i * xc


def _scan8(a, b, h_prev, row):
    for d in (1, 2, 4):
        keep = row >= d
        a_s = jnp.where(keep, pltpu.roll(a, d, 0), 1.0)
        b_s = jnp.where(keep, pltpu.roll(b, d, 0), 0.0)
        b = b + a * b_s
        a = a * a_s
    y = b + a * h_prev
    return y, y[7:8, :]


def _rglru_prompt_kernel(conv_w, z_ref, cw_ref, cb_ref, wa_ref, wx_ref, ba_ref, bx_ref, lam_ref,
                         y_ref, h_ref, nb_ref, xp_ref, a_ref, b_ref):
    t_len, c = z_ref.shape
    pad = 8
    xp_ref[0:pad, :] = jnp.zeros((pad, c), F32)
    xp_ref[pad:pad + t_len, :] = z_ref[...].astype(F32)
    nb_ref[...] = xp_ref[pad + t_len - (conv_w - 1):pad + t_len, :]

    wa_bf = wa_ref[...].astype(BF16)
    wx_bf = wx_ref[...].astype(BF16)
    logsig = jax.nn.log_sigmoid(lam_ref[...])
    ba = ba_ref[...]
    bx = bx_ref[...]
    cb = cb_ref[...]
    row8 = lax.broadcasted_iota(jnp.int32, (8, c), 0)
    ch = min(SCAN_CHUNK, t_len)

    h = jnp.zeros((1, c), F32)
    for ci in range(t_len // ch):
        base = pad + ci * ch
        xc = cb
        for k in range(conv_w):
            off = base - (conv_w - 1) + k
            xc = xc + cw_ref[k:k + 1, :] * xp_ref[off:off + ch, :]
        a, mult, ix = _lru_gates(xc, wa_bf, wx_bf, ba, bx, logsig)
        if ci == 0:
            rows = lax.broadcasted_iota(jnp.int32, (ch, c), 0)
            mult = jnp.where(rows == 0, 1.0, mult)
        a_ref[...] = a
        b_ref[...] = mult * ix

        def group(g, h, ci=ci):
            r0 = pl.multiple_of(g * BF16_ROWS, BF16_ROWS)
            a16 = a_ref[pl.ds(r0, BF16_ROWS), :]
            b16 = b_ref[pl.ds(r0, BF16_ROWS), :]
            y0, h = _scan8(a16[0:8], b16[0:8], h, row8)
            y1, h = _scan8(a16[8:16], b16[8:16], h, row8)
            y_ref[pl.ds(ci * ch + r0, BF16_ROWS), :] = jnp.concatenate([y0, y1], 0).astype(y_ref.dtype)
            return h

        h = lax.fori_loop(0, ch // BF16_ROWS, group, h)
    h_ref[...] = h


def _rglru_prompt(z, p, layer, batch, seq, rb):
    d_rnn = p["conv_b"].shape[-1]
    conv_w = p["conv_w"].shape[1]
    nb = d_rnn // rb
    vec = lambda: pl.BlockSpec((None, 1, rb), lambda b, n: (layer, 0, n))
    blk = lambda: pl.BlockSpec((None, None, rb, rb), lambda b, n: (layer, n, 0, 0))
    ch = min(SCAN_CHUNK, seq)
    return pl.pallas_call(
        functools.partial(_rglru_prompt_kernel, conv_w),
        grid=(batch, nb),
        in_specs=[pl.BlockSpec((seq, rb), lambda b, n: (b, n)),
                  pl.BlockSpec((None, conv_w, rb), lambda b, n: (layer, 0, n)),
                  vec(), blk(), blk(), vec(), vec(), vec()],
        out_specs=[pl.BlockSpec((seq, rb), lambda b, n: (b, n)),
                   pl.BlockSpec((None, 1, rb), lambda b, n: (b, 0, n)),
                   pl.BlockSpec((None, conv_w - 1, rb), lambda b, n: (b, 0, n))],
        out_shape=[jax.ShapeDtypeStruct((batch * seq, d_rnn), BF16),
                   jax.ShapeDtypeStruct((batch, 1, d_rnn), F32),
                   jax.ShapeDtypeStruct((batch, conv_w - 1, d_rnn), F32)],
        scratch_shapes=[pltpu.VMEM((seq + 8, rb), F32),
                        pltpu.VMEM((ch, rb), F32), pltpu.VMEM((ch, rb), F32)],
        compiler_params=_params("parallel", "parallel"),
        name="rglru_prompt",
    )(z, p["conv_w"], p["conv_b"], p["lru_wa"], p["lru_wx"], p["lru_ba"], p["lru_bx"], p["lru_lambda"])


def _rglru_sample_kernel(conv_w, n_t, z_ref, buf_ref, h0_ref, cw_ref, cb_ref, wa_ref, wx_ref,
                         ba_ref, bx_ref, lam_ref, y_ref, h_ref, nb_ref):
    rows, c = z_ref.shape
    bsz = rows // n_t
    xr = z_ref[...].astype(F32)
    slabs = [buf_ref[k] for k in range(conv_w - 1)] + [xr[t * bsz:(t + 1) * bsz] for t in range(n_t)]
    cb = cb_ref[...]
    xcs = []
    for t in range(n_t):
        xc = cb
        for k in range(conv_w):
            xc = xc + cw_ref[k:k + 1, :] * slabs[t + k]
        xcs.append(xc)
    xc = jnp.concatenate(xcs, 0)
    a, mult, ix = _lru_gates(xc, wa_ref[...].astype(BF16), wx_ref[...].astype(BF16),
                             ba_ref[...], bx_ref[...], jax.nn.log_sigmoid(lam_ref[...]))
    b = mult * ix
    h = h0_ref[...]
    ys = []
    for t in range(n_t):
        h = a[t * bsz:(t + 1) * bsz] * h + b[t * bsz:(t + 1) * bsz]
        ys.append(h)
    y_ref[...] = jnp.concatenate(ys, 0).astype(y_ref.dtype)
    h_ref[...] = h
    for k in range(conv_w - 1):
        nb_ref[k] = slabs[n_t + k]


def _rglru_sample(z, buf_t, h0, p, layer, n_t, bsz, rb, row_block):
    d_rnn = p["conv_b"].shape[-1]
    conv_w = p["conv_w"].shape[1]
    assert PAST_LEN > 0 and n_t >= conv_w - 1
    rows = n_t * bsz
    vec = lambda: pl.BlockSpec((None, 1, rb), lambda n: (layer, 0, n))
    blk = lambda: pl.BlockSpec((None, None, rb, rb), lambda n: (layer, n, 0, 0))
    return pl.pallas_call(
        functools.partial(_rglru_sample_kernel, conv_w, n_t),
        grid=(d_rnn // rb,),
        in_specs=[pl.BlockSpec((rows, rb), lambda n: (row_block, n)),
                  pl.BlockSpec((None, conv_w - 1, bsz, rb), lambda n: (layer, 0, 0, n)),
                  pl.BlockSpec((None, bsz, rb), lambda n: (layer, 0, n)),
                  pl.BlockSpec((None, conv_w, rb), lambda n: (layer, 0, n)),
                  vec(), blk(), blk(), vec(), vec(), vec()],
        out_specs=[pl.BlockSpec((rows, rb), lambda n: (0, n)),
                   pl.BlockSpec((bsz, rb), lambda n: (0, n)),
                   pl.BlockSpec((conv_w - 1, bsz, rb), lambda n: (0, 0, n))],
        out_shape=[jax.ShapeDtypeStruct((rows, d_rnn), BF16),
                   jax.ShapeDtypeStruct((bsz, d_rnn), F32),
                   jax.ShapeDtypeStruct((conv_w - 1, bsz, d_rnn), F32)],
        compiler_params=_params("parallel"),
        name="rglru_sample",
    )(z, buf_t, h0, p["conv_w"], p["conv_b"], p["lru_wa"], p["lru_wx"], p["lru_ba"], p["lru_bx"],
      p["lru_lambda"])


def _layernorm(v, g, b):
    mu = jnp.mean(v, axis=-1, keepdims=True)
    var = jnp.mean(jnp.square(v - mu), axis=-1, keepdims=True)
    return (v - mu) * lax.rsqrt(var + EPS) * g + b


def _sgu_prompt_kernel(u_ref, v_ref, g_ref, b_ref, ws_ref, bst_ref, y_ref):
    rows = u_ref.shape[0]
    n_g, chunk, _ = ws_ref.shape
    sg = u_ref.shape[1] // n_g
    vn = _layernorm(v_ref[...].astype(F32), g_ref[...], b_ref[...]).astype(BF16)
    tri = (lax.broadcasted_iota(jnp.int32, (chunk, chunk), 0)
           >= lax.broadcasted_iota(jnp.int32, (chunk, chunk), 1))
    for g in range(n_g):
        w = jnp.where(tri, ws_ref[g], 0.0).astype(BF16)
        bias = bst_ref[:, g:g + 1]
        for ci in range(rows // chunk):
            rs = slice(ci * chunk, (ci + 1) * chunk)
            cs = slice(g * sg, (g + 1) * sg)
            s = jnp.dot(w, vn[rs, cs], preferred_element_type=F32) + bias
            y_ref[rs, cs] = (u_ref[rs, cs].astype(F32) * s).astype(y_ref.dtype)


def _sgu_prompt(z, p, layer, n_prompt, col_u, col_v):
    d_chunk = p["sgu_ln_g"].shape[-1]
    n_g, chunk = p["sgu_ws"].shape[1], p["sgu_ws"].shape[2]
    tm = _tile(n_prompt, POS_TILE, chunk)
    vec = lambda: pl.BlockSpec((None, 1, d_chunk), lambda i: (layer, 0, 0))
    return pl.pallas_call(
        _sgu_prompt_kernel,
        grid=(n_prompt // tm,),
        in_specs=[pl.BlockSpec((tm, d_chunk), lambda i: (i, col_u // d_chunk)),
                  pl.BlockSpec((tm, d_chunk), lambda i: (i, col_v // d_chunk)),
                  vec(), vec(),
                  pl.BlockSpec((None, n_g, chunk, chunk), lambda i: (layer, 0, 0, 0)),
                  pl.BlockSpec((None, chunk, n_g), lambda i: (layer, 0, 0))],
        out_specs=pl.BlockSpec((tm, d_chunk), lambda i: (i, 0)),
        out_shape=jax.ShapeDtypeStruct((n_prompt, d_chunk), BF16),
        compiler_params=_params("parallel"),
        name="sgu_prompt",
    )(z, z, p["sgu_ln_g"], p["sgu_ln_b"], p["sgu_ws"], p["sgu_bs_t"])


def _sgu_sample_kernel(n_t, u_ref, v_ref, g_ref, b_ref, coef_ref, bias_ref, y_ref, vn_ref):
    bsz = u_ref.shape[0] // n_t
    vn = _layernorm(v_ref[...].astype(F32), g_ref[...], b_ref[...])
    vn_ref[...] = vn
    vn = vn.astype(BF16).astype(F32)
    coef = coef_ref[...].astype(BF16).astype(F32)
    for t in range(n_t):
        s = bias_ref[t:t + 1, :]
        for k in range(t + 1):
            s = s + coef[t * n_t + k:t * n_t + k + 1, :] * vn[k * bsz:(k + 1) * bsz]
        rs = slice(t * bsz, (t + 1) * bsz)
        y_ref[rs, :] = (u_ref[rs, :].astype(F32) * s).astype(y_ref.dtype)


def _sgu_sample(z, p, layer, n_t, bsz, row_block, col_u, col_v):
    d_chunk = p["sgu_ln_g"].shape[-1]
    rows = n_t * bsz
    vec = lambda: pl.BlockSpec((None, 1, d_chunk), lambda i: (layer, 0, 0))
    return pl.pallas_call(
        functools.partial(_sgu_sample_kernel, n_t),
        grid=(1,),
        in_specs=[pl.BlockSpec((rows, d_chunk), lambda i: (row_block, col_u // d_chunk)),
                  pl.BlockSpec((rows, d_chunk), lambda i: (row_block, col_v // d_chunk)),
                  vec(), vec(),
                  pl.BlockSpec((None, n_t * n_t, d_chunk), lambda i: (layer, 0, 0)),
                  pl.BlockSpec((None, n_t, d_chunk), lambda i: (layer, 0, 0))],
        out_specs=[pl.BlockSpec((rows, d_chunk), lambda i: (0, 0)),
                   pl.BlockSpec((rows, d_chunk), lambda i: (0, 0))],
        out_shape=[jax.ShapeDtypeStruct((rows, d_chunk), BF16),
                   jax.ShapeDtypeStruct((rows, d_chunk), F32)],
        compiler_params=_params("arbitrary"),
        name="sgu_sample",
    )(z, z, p["sgu_ln_g"], p["sgu_ln_b"], p["sgu_coef"], p["sgu_bias"])


def _router_kernel(n_e, x_ref, g_ref, wr_ref, idx_ref, wt_ref):
    xn = _rms(x_ref[...], g_ref[...])
    logits = jnp.dot(xn.astype(BF16), wr_ref[...].astype(BF16), preferred_element_type=F32)
    lane = lax.broadcasted_iota(jnp.int32, logits.shape, 1)
    neg = jnp.float32(-jnp.inf)
    l1 = jnp.where(lane < n_e, logits, neg)
    m1 = jnp.max(l1, axis=-1, keepdims=True)
    i1 = jnp.min(jnp.where(l1 == m1, lane, LANES), axis=-1, keepdims=True)
    l2 = jnp.where(lane == i1, neg, l1)
    m2 = jnp.max(l2, axis=-1, keepdims=True)
    i2 = jnp.min(jnp.where(l2 == m2, lane, LANES), axis=-1, keepdims=True)
    e2 = jnp.exp(m2 - m1)
    w1 = 1.0 / (1.0 + e2)
    w2 = e2 / (1.0 + e2)
    idx_ref[...] = jnp.where(lane == 0, i1, jnp.where(lane == 1, i2, 0))
    wt_ref[...] = jnp.where(lane == 0, w1, jnp.where(lane == 1, w2, 0.0))


def _router(x, g3, layer, wr_pad, jm, n_e):
    m, d = x.shape
    tm = _tile(m, ROW_TILE_HALF, 8)
    return pl.pallas_call(
        functools.partial(_router_kernel, n_e),
        grid=(m // tm,),
        in_specs=[pl.BlockSpec((tm, d), lambda i: (i, 0)),
                  pl.BlockSpec((None, 1, d), lambda i: (layer, 0, 0)),
                  pl.BlockSpec((None, d, LANES), lambda i: (jm, 0, 0))],
        out_specs=[pl.BlockSpec((tm, LANES), lambda i: (i, 0)),
                   pl.BlockSpec((tm, LANES), lambda i: (i, 0))],
        out_shape=[jax.ShapeDtypeStruct((m, LANES), jnp.int32),
                   jax.ShapeDtypeStruct((m, LANES), F32)],
        compiler_params=_params("parallel"),
        name="moe_router",
    )(x, g3, wr_pad)


def _route_meta(idx2, n_e, te_rows, n_tiles):
    e = idx2.reshape(-1)
    n_pairs = e.shape[0]
    onehot = (e[:, None] == jnp.arange(n_e, dtype=jnp.int32)[None, :]).astype(jnp.int32)
    rank = jnp.sum((jnp.cumsum(onehot, axis=0) - onehot) * onehot, axis=1)
    count = jnp.sum(onehot, axis=0)
    tiles_e = (count + te_rows - 1) // te_rows
    tile_end = jnp.cumsum(tiles_e)
    tile_start = tile_end - tiles_e
    pos = tile_start[e] * te_rows + rank
    n_used = tile_end[-1:].astype(jnp.int32)
    token_of = jnp.zeros((n_tiles * te_rows,), jnp.int32).at[pos].set(
        jnp.arange(n_pairs, dtype=jnp.int32) // TOP_K)
    tile_blk = jnp.minimum(jnp.arange(n_tiles, dtype=jnp.int32), n_used[0] - 1)
    tile_e = jnp.minimum(jnp.sum(tile_blk[:, None] >= tile_end[None, :], axis=1), n_e - 1).astype(jnp.int32)
    return pos.reshape(-1, TOP_K).astype(jnp.int32), token_of, tile_e, tile_blk, n_used


def _row_copy(src_hbm, idx_ref, dst_ref, sem, r):
    return pltpu.make_async_copy(src_hbm.at[pl.ds(idx_ref[0, r], 1)], dst_ref.at[pl.ds(r, 1)], sem)


def _start_rows(src_hbm, idx_ref, dst_ref, sem):
    def start(k, c):
        for p in range(DMA_PRIORITIES):
            _row_copy(src_hbm, idx_ref, dst_ref, sem, DMA_PRIORITIES * k + p).start(priority=p)
        return c

    lax.fori_loop(0, dst_ref.shape[0] // DMA_PRIORITIES, start, 0, unroll=DMA_UNROLL // DMA_PRIORITIES)


def _wait_rows(src_hbm, idx_ref, dst_ref, sem):
    def wait(r, c):
        _row_copy(src_hbm, idx_ref, dst_ref, sem, r).wait()
        return c

    lax.fori_loop(0, dst_ref.shape[0], wait, 0, unroll=DMA_UNROLL)


def _moe_gather_kernel(nu_ref, tok_ref, tok_next_ref, g_ref, x_hbm, o_ref, buf_ref, sem):
    t = pl.program_id(0)
    n_used = nu_ref[0]
    slot = t % 2

    @pl.when(t == 0)
    def _():
        _start_rows(x_hbm, tok_ref, buf_ref.at[0], sem.at[0])

    @pl.when(t + 1 < n_used)
    def _():
        _start_rows(x_hbm, tok_next_ref, buf_ref.at[1 - slot], sem.at[1 - slot])

    @pl.when(t < n_used)
    def _():
        _wait_rows(x_hbm, tok_ref, buf_ref.at[slot], sem.at[slot])
        o_ref[...] = _rms(buf_ref[slot], g_ref[...]).astype(o_ref.dtype)

    @pl.when(t >= n_used)
    def _():
        o_ref[...] = jnp.zeros_like(o_ref)


def _moe_gather(x, g3, layer, token_of, tile_blk, n_used, te_rows):
    d = x.shape[1]
    n_tiles = tile_blk.shape[0]
    tok3 = token_of.reshape(n_tiles, 1, te_rows)
    tok_spec = lambda step: pl.BlockSpec(
        (None, 1, te_rows), lambda t, nu, tb: (tb[jnp.minimum(t + step, n_tiles - 1)], 0, 0),
        memory_space=pltpu.SMEM)
    grid_spec = pltpu.PrefetchScalarGridSpec(
        num_scalar_prefetch=2,
        grid=(n_tiles,),
        in_specs=[tok_spec(0), tok_spec(1),
                  pl.BlockSpec((None, 1, d), lambda t, nu, tb: (layer, 0, 0)),
                  pl.BlockSpec(memory_space=pl.ANY)],
        out_specs=pl.BlockSpec((te_rows, d), lambda t, nu, tb: (t, 0)),
        scratch_shapes=[pltpu.VMEM((2, te_rows, d), F32), pltpu.SemaphoreType.DMA((2,))],
    )

    def kern(nu_ref, tb_ref, tok_ref, tok_next_ref, g_ref, x_hbm, o_ref, buf_ref, sem):
        del tb_ref
        _moe_gather_kernel(nu_ref, tok_ref, tok_next_ref, g_ref, x_hbm, o_ref, buf_ref, sem)

    return pl.pallas_call(
        kern,
        grid_spec=grid_spec,
        out_shape=jax.ShapeDtypeStruct((n_tiles * te_rows, d), BF16),
        compiler_params=_params("arbitrary"),
        name="moe_gather",
    )(n_used, tile_blk, tok3, tok3, g3, x)


def _first_tile_of_expert(te_ref, t):
    return (t == 0) | (te_ref[t] != te_ref[jnp.maximum(t - 1, 0)])


def _moe_up_kernel(te_ref, tb_ref, nu_ref, x_ref, wg_ref, wu_ref, o_ref, wgbf_ref, wubf_ref):
    del tb_ref
    t = pl.program_id(1)
    used = t < nu_ref[0]

    @pl.when(used)
    def _():
        @pl.when(_first_tile_of_expert(te_ref, t))
        def _():
            wgbf_ref[...] = wg_ref[...].astype(BF16)
            wubf_ref[...] = wu_ref[...].astype(BF16)

        x = x_ref[...]
        g = jnp.dot(x, wgbf_ref[...], preferred_element_type=F32)
        u = jnp.dot(x, wubf_ref[...], preferred_element_type=F32)
        o_ref[...] = (jax.nn.silu(g) * u).astype(o_ref.dtype)

    @pl.when(jnp.logical_not(used))
    def _():
        o_ref[...] = jnp.zeros_like(o_ref)


def _moe_up(xs, wg, wu, jm, tile_e, tile_blk, n_used, te_rows):
    d = xs.shape[1]
    f = wg.shape[-1]
    n_tiles = tile_blk.shape[0]
    tf = _tile(f, COL_TILE, LANES)
    wspec = lambda: pl.BlockSpec((None, None, d, tf), lambda j, t, te, tb, nu: (jm, te[t], 0, j))
    grid_spec = pltpu.PrefetchScalarGridSpec(
        num_scalar_prefetch=3,
        grid=(f // tf, n_tiles),
        in_specs=[pl.BlockSpec((te_rows, d), lambda j, t, te, tb, nu: (tb[t], 0)), wspec(), wspec()],
        out_specs=pl.BlockSpec((te_rows, tf), lambda j, t, te, tb, nu: (t, j)),
        scratch_shapes=[pltpu.VMEM((d, tf), BF16), pltpu.VMEM((d, tf), BF16)],
    )
    return pl.pallas_call(
        _moe_up_kernel,
        grid_spec=grid_spec,
        out_shape=jax.ShapeDtypeStruct((xs.shape[0], f), BF16),
        compiler_params=_params("arbitrary", "arbitrary"),
        name="moe_gate_up",
    )(tile_e, tile_blk, n_used, xs, wg, wu)


def _moe_down_kernel(te_ref, tb_ref, nu_ref, h_ref, wd_ref, o_ref, wdbf_ref):
    del tb_ref
    t = pl.program_id(1)
    used = t < nu_ref[0]

    @pl.when(used)
    def _():
        @pl.when(_first_tile_of_expert(te_ref, t))
        def _():
            wdbf_ref[...] = wd_ref[...].astype(BF16)

        o_ref[...] = jnp.dot(h_ref[...], wdbf_ref[...], preferred_element_type=F32)

    @pl.when(jnp.logical_not(used))
    def _():
        o_ref[...] = jnp.zeros_like(o_ref)


def _moe_down(hs, wd, jm, tile_e, tile_blk, n_used, te_rows):
    f = hs.shape[1]
    d = wd.shape[-1]
    n_tiles = tile_blk.shape[0]
    tn = _tile(d, COL_TILE, LANES)
    grid_spec = pltpu.PrefetchScalarGridSpec(
        num_scalar_prefetch=3,
        grid=(d // tn, n_tiles),
        in_specs=[pl.BlockSpec((te_rows, f), lambda j, t, te, tb, nu: (tb[t], 0)),
                  pl.BlockSpec((None, None, f, tn), lambda j, t, te, tb, nu: (jm, te[t], 0, j))],
        out_specs=pl.BlockSpec((te_rows, tn), lambda j, t, te, tb, nu: (t, j)),
        scratch_shapes=[pltpu.VMEM((f, tn), BF16)],
    )
    return pl.pallas_call(
        _moe_down_kernel,
        grid_spec=grid_spec,
        out_shape=jax.ShapeDtypeStruct((hs.shape[0], d), F32),
        compiler_params=_params("arbitrary", "arbitrary"),
        name="moe_down",
    )(tile_e, tile_blk, n_used, hs, wd)


def _moe_combine_kernel(p1_ref, p2_ref, p1_next_ref, p2_next_ref, x_ref, wt_ref, ys_hbm, o_ref,
                        b1_ref, b2_ref, sem):
    i = pl.program_id(0)
    slot = i % 2

    @pl.when(i == 0)
    def _():
        _start_rows(ys_hbm, p1_ref, b1_ref.at[0], sem.at[0, 0])
        _start_rows(ys_hbm, p2_ref, b2_ref.at[0], sem.at[1, 0])

    @pl.when(i + 1 < pl.num_programs(0))
    def _():
        _start_rows(ys_hbm, p1_next_ref, b1_ref.at[1 - slot], sem.at[0, 1 - slot])
        _start_rows(ys_hbm, p2_next_ref, b2_ref.at[1 - slot], sem.at[1, 1 - slot])

    _wait_rows(ys_hbm, p1_ref, b1_ref.at[slot], sem.at[0, slot])
    _wait_rows(ys_hbm, p2_ref, b2_ref.at[slot], sem.at[1, slot])
    w = wt_ref[...]
    o_ref[...] = x_ref[...] + w[:, 0:1] * b1_ref[slot] + w[:, 1:2] * b2_ref[slot]


def _moe_combine(x, wt, ys, pos):
    m, d = x.shape
    tg = _tile(m, GATHER_TILE, 8)
    n_steps = m // tg
    p1 = pos[:, 0].reshape(n_steps, 1, tg)
    p2 = pos[:, 1].reshape(n_steps, 1, tg)
    pspec = lambda step: pl.BlockSpec((None, 1, tg), lambda i: (jnp.minimum(i + step, n_steps - 1), 0, 0),
                                      memory_space=pltpu.SMEM)
    return pl.pallas_call(
        _moe_combine_kernel,
        grid=(n_steps,),
        in_specs=[pspec(0), pspec(0), pspec(1), pspec(1),
                  pl.BlockSpec((tg, d), lambda i: (i, 0)),
                  pl.BlockSpec((tg, LANES), lambda i: (i, 0)),
                  pl.BlockSpec(memory_space=pl.ANY)],
        out_specs=pl.BlockSpec((tg, d), lambda i: (i, 0)),
        out_shape=jax.ShapeDtypeStruct((m, d), F32),
        scratch_shapes=[pltpu.VMEM((2, tg, d), F32), pltpu.VMEM((2, tg, d), F32),
                        pltpu.SemaphoreType.DMA((2, 2))],
        compiler_params=_params("arbitrary"),
        name="moe_combine",
    )(p1, p2, p1, p2, x, wt, ys)


def kernel(x_prompt, x_sample, state_rglru_h, state_conv, norm_mix_g, w_in, conv_w, conv_b, lru_wa, lru_ba, lru_wx, lru_bx, lru_lambda, sgu_ln_g, sgu_ln_b, sgu_ws, sgu_bs, w_br_a, w_br_b, w_out, norm_ffn_g, ffn_wg, ffn_wu, ffn_wd, router_w, moe_wg, moe_wu, moe_wd, norm_final_g):
    batch, seq, d_model = x_prompt.shape
    bsz, n_t, _ = x_sample.shape
    depth = w_in.shape[0]
    d_rnn = conv_b.shape[-1]
    d_chunk = sgu_ln_g.shape[-1]
    rb = lru_wa.shape[-1]
    n_g = sgu_ws.shape[1]
    sg = d_chunk // n_g
    n_e = router_w.shape[-1]
    n_prompt = batch * seq
    n_sample = n_t * bsz
    m = n_prompt + n_sample
    assert n_prompt % n_sample == 0 and bsz % BF16_ROWS == 0 and n_e <= LANES
    sample_block = n_prompt // n_sample
    col_u, col_v, col_gate = d_rnn, d_rnn + d_chunk, d_rnn + 2 * d_chunk

    x = jnp.concatenate([x_prompt.reshape(n_prompt, d_model),
                         jnp.swapaxes(x_sample, 0, 1).reshape(n_sample, d_model)], axis=0)

    row3 = lambda a: a.reshape(a.shape[0], 1, -1)
    p = dict(
        conv_w=conv_w, conv_b=row3(conv_b), lru_wa=lru_wa, lru_wx=lru_wx,
        lru_ba=row3(lru_ba), lru_bx=row3(lru_bx), lru_lambda=row3(lru_lambda),
        sgu_ln_g=row3(sgu_ln_g), sgu_ln_b=row3(sgu_ln_b), sgu_ws=sgu_ws,
        sgu_bs_t=jnp.swapaxes(sgu_bs, 1, 2),
        sgu_coef=jnp.repeat(jnp.transpose(sgu_ws[:, :, :n_t, :n_t], (0, 2, 3, 1)).reshape(depth, n_t * n_t, n_g),
                            sg, axis=-1),
        sgu_bias=jnp.repeat(jnp.swapaxes(sgu_bs[:, :, :n_t], 1, 2), sg, axis=-1),
    )
    g_mix, g_ffn = row3(norm_mix_g), row3(norm_ffn_g)
    buf_t = jnp.swapaxes(state_conv, 1, 2)
    wr_pad = jnp.pad(router_w, ((0, 0), (0, 0), (0, LANES - n_e)))

    te_rows = _tile(m, ROW_TILE_HALF, BF16_ROWS)
    n_pairs = m * TOP_K
    n_tiles = (n_pairs + n_e * (te_rows - 1) + te_rows - 1) // te_rows
    tm_norm = _tile(m, ROW_TILE_HALF, BF16_ROWS)

    h_p, c_p, h_s, c_s, v_s = [], [], [], [], []
    for l in range(depth):
        xn = _rmsnorm(x, g_mix, l, tm_norm, BF16)
        z = _inproj(xn, w_in, l, d_rnn, d_chunk)

        ya_p, h_pl, c_pl = _rglru_prompt(z, p, l, batch, seq, rb)
        ya_s, h_sl, c_sl = _rglru_sample(z, buf_t, state_rglru_h, p, l, n_t, bsz, rb, sample_block)
        yb_p = _sgu_prompt(z, p, l, n_prompt, col_u, col_v)
        yb_s, v_sl = _sgu_sample(z, p, l, n_t, bsz, sample_block, col_u, col_v)
        h_p.append(h_pl.reshape(batch, d_rnn))
        c_p.append(c_pl)
        h_s.append(h_sl)
        c_s.append(jnp.swapaxes(c_sl, 0, 1))
        v_s.append(jnp.swapaxes(v_sl.reshape(n_t, bsz, d_chunk), 0, 1))

        mix = _merge(ya_p, yb_p, ya_s, yb_s, z, w_br_a, w_br_b, l, col_gate)
        x = _matmul_acc(mix, w_out, l, x)

        j = l // 2
        if l % 2 == 0:
            xn = _rmsnorm(x, g_ffn, l, tm_norm, BF16)
            hid = _gateup(xn, ffn_wg, ffn_wu, j)
            for kb in range(ffn_wd.shape[1] // min(K_CHUNK, ffn_wd.shape[1])):
                x = _matmul_acc(hid, ffn_wd, j, x, kb)
        else:
            idx, wt = _router(x, g_ffn, l, wr_pad, j, n_e)
            pos, token_of, tile_e, tile_blk, n_used = _route_meta(idx[:, :TOP_K], n_e, te_rows, n_tiles)
            xs = _moe_gather(x, g_ffn, l, token_of, tile_blk, n_used, te_rows)
            hs = _moe_up(xs, moe_wg, moe_wu, j, tile_e, tile_blk, n_used, te_rows)
            ys = _moe_down(hs, moe_wd, j, tile_e, tile_blk, n_used, te_rows)
            x = _moe_combine(x, wt, ys, pos)

    g_fin = norm_final_g.reshape(1, 1, d_model)
    tm_fin = _tile(n_sample, POS_TILE, 8)
    y_prompt = _rmsnorm(x, g_fin, 0, tm_fin, F32, 0, n_prompt).reshape(batch, seq, d_model)
    y_sample = _rmsnorm(x, g_fin, 0, tm_fin, F32, n_prompt // tm_fin, n_sample)
    y_sample = jnp.swapaxes(y_sample.reshape(n_t, bsz, d_model), 0, 1)
    return (y_prompt, y_sample, jnp.stack(h_p), jnp.stack(c_p), jnp.stack(h_s), jnp.stack(c_s),
            jnp.stack(v_s))
```
